```python
import math, functools
import jax, jax.numpy as jnp
from jax import lax
import numpy as np

D_MODEL = 1024
BATCH = 8
SEQ = 2048
DEPTH = 2
DEC_BATCH = 128
DEC_SEQ = 4
PAST_LEN = 2048
PAGE_SIZE = 128

N_BRANCH = 4
BRANCH_W = D_MODEL // 2
POOL_WINDOWS = (2, 4, 8, 16)
POOL_GROUPS = len(POOL_WINDOWS)
POOL_GC = BRANCH_W // POOL_GROUPS
POOL_BUF = max(POOL_WINDOWS) - 1
SCONV_K = 3
CCONV_K = 31
HEAD_DK = 64
HEAD_DV = 2 * HEAD_DK
N_HEADS = BRANCH_W // HEAD_DV
ATTN_SCALE = HEAD_DK ** -0.5
ROPE_THETA = 10000.0
Q_BLOCK = 128
EPS = 1e-6
IN_SIZES = (BRANCH_W, 3 * BRANCH_W, 2 * BRANCH_W, N_HEADS * 2 * HEAD_DK, N_HEADS * 2 * HEAD_DK, N_HEADS * HEAD_DV, N_BRANCH * BRANCH_W, N_BRANCH * D_MODEL)
N_IN = sum(IN_SIZES)

kernel_name = 'hybrid_pool_conv_diffattn_step'


def _split(t, sizes):
    out, o = [], 0
    for s in sizes:
        out.append(t[..., o:o + s])
        o += s
    return out


def _rms(x, g):
    xf = x.astype(jnp.float32)
    y = xf * lax.rsqrt(jnp.mean(xf * xf, axis=-1, keepdims=True) + EPS)
    return (y * g.astype(jnp.float32)).astype(x.dtype)


def _layernorm(x, g, b):
    xf = x.astype(jnp.float32)
    mu = jnp.mean(xf, axis=-1, keepdims=True)
    xc = xf - mu
    y = xc * lax.rsqrt(jnp.mean(xc * xc, axis=-1, keepdims=True) + EPS)
    return (y * g.astype(jnp.float32) + b.astype(jnp.float32)).astype(x.dtype)


def _rope(t, pos):
    half = HEAD_DK // 2
    inv = jnp.power(ROPE_THETA, -jnp.arange(half, dtype=jnp.float32) / half)
    ang = pos.astype(jnp.float32)[:, None] * inv[None, :]
    cos = jnp.cos(ang)[None, :, None, None, :]
    sin = jnp.sin(ang)[None, :, None, None, :]
    tf = t.astype(jnp.float32)
    t1, t2 = tf[..., :half], tf[..., half:]
    return jnp.concatenate([t1 * cos - t2 * sin, t2 * cos + t1 * sin], axis=-1).astype(t.dtype)


def _causal_dwconv(u, prefix, w):
    k_w, ch = w.shape
    uu = jnp.concatenate([prefix.astype(u.dtype), u], axis=1)
    out = lax.conv_general_dilated(uu, w[:, None, :].astype(u.dtype), window_strides=(1,), padding='VALID',
                                   dimension_numbers=('NWC', 'WIO', 'NWC'), feature_group_count=ch)
    return out, uu[:, uu.shape[1] - (k_w - 1):]


def _pool_mixer(u, prefix, pos, w_pool, pool_scale):
    b, s, w_ch = u.shape
    uu = jnp.concatenate([prefix.astype(u.dtype), u], axis=1)
    cs = jnp.cumsum(uu.astype(jnp.float32), axis=1)
    cs = jnp.concatenate([jnp.zeros((b, 1, w_ch), jnp.float32), cs], axis=1)
    end = cs[:, POOL_BUF + 1:]
    means = []
    for g, win in enumerate(POOL_WINDOWS):
        sl = slice(g * POOL_GC, (g + 1) * POOL_GC)
        start = cs[:, POOL_BUF + 1 - win:POOL_BUF + 1 - win + s, sl]
        cnt = jnp.minimum(win, pos + 1).astype(jnp.float32)[None, :, None]
        means.append((end[..., sl] - start) / cnt)
    mean = jnp.stack(means, axis=2)
    d = (mean - u.reshape(b, s, POOL_GROUPS, POOL_GC).astype(jnp.float32)).astype(u.dtype)
    y = jnp.einsum('bsgc,gce->bsge', d, w_pool).reshape(b, s, w_ch) * pool_scale
    return y, uu[:, uu.shape[1] - POOL_BUF:]


def _diff_weights(s, lam):
    p = jax.nn.softmax(s, axis=-1)
    return p[:, :, 0] - lam * p[:, :, 1]


def _attend_prompt(q, k, v, lam):
    b, s = q.shape[:2]
    nb = s // Q_BLOCK
    qb = jnp.moveaxis(q.reshape(b, nb, Q_BLOCK, N_HEADS, 2, HEAD_DK), 1, 0)
    kpos = jnp.arange(s)

    def block(args):
        qi, bi = args
        sc = jnp.einsum('bqhjd,bkhjd->bhjqk', qi, k).astype(jnp.float32) * ATTN_SCALE
        qpos = bi * Q_BLOCK + jnp.arange(Q_BLOCK)
        sc = jnp.where(kpos[None, :] <= qpos[:, None], sc, -jnp.inf)
        wts = _diff_weights(sc, lam).astype(v.dtype)
        return jnp.einsum('bhqk,bkhe->bqhe', wts, v)

    o = lax.map(block, (qb, jnp.arange(nb)))
    return jnp.moveaxis(o, 0, 1).reshape(b, s, N_HEADS, HEAD_DV)


def _attend_sample(q, k, v, lam, k_past, v_past):
    t = q.shape[1]
    p_len = k_past.shape[1]
    s_past = jnp.einsum('bqhjd,bkhjd->bhjqk', q, k_past).astype(jnp.float32) * ATTN_SCALE
    s_new = jnp.einsum('bqhjd,bkhjd->bhjqk', q, k).astype(jnp.float32) * ATTN_SCALE
    causal = jnp.tril(jnp.ones((t, t), dtype=bool))
    sc = jnp.concatenate([s_past, jnp.where(causal, s_new, -jnp.inf)], axis=-1)
    wts = _diff_weights(sc, lam).astype(v.dtype)
    return (jnp.einsum('bhqk,bkhe->bqhe', wts[..., :p_len], v_past)
            + jnp.einsum('bhqk,bkhe->bqhe', wts[..., p_len:], v))


def _layer(x, c, pos, pool_pre, sconv_pre, cconv_pre, attend, lam_init,
           w_ada, b_ada, g_pre, g_post, w_in, w_pool, pool_scale, w_sconv, w_cconv,
           b_cconv, g_cnorm, b_cnorm, lambda_qk, g_subln, w_branch, w_o):
    b, s, _ = x.shape
    mod = jax.nn.silu(c) @ w_ada + b_ada
    shift, scale, gate = jnp.split(mod, 3, axis=-1)
    h = _rms(x, g_pre) * (1.0 + scale[:, None]) + shift[:, None]
    u_a, bch, glu, q, k, v, mgate, mrg = _split(h @ w_in, IN_SIZES)
    y_a, pool_new = _pool_mixer(u_a, pool_pre, pos, w_pool, pool_scale)
    b_g, c_g, h_b = _split(bch, (BRANCH_W, BRANCH_W, BRANCH_W))
    conv_b, sconv_new = _causal_dwconv(c_g * h_b, sconv_pre, w_sconv)
    y_b = b_g * conv_b
    val, gl = _split(glu, (BRANCH_W, BRANCH_W))
    conv_c, cconv_new = _causal_dwconv(val * jax.nn.sigmoid(gl), cconv_pre, w_cconv)
    y_c = jax.nn.silu(_layernorm(conv_c + b_cconv, g_cnorm, b_cnorm))
    q = _rope(q.reshape(b, s, N_HEADS, 2, HEAD_DK), pos)
    k = _rope(k.reshape(b, s, N_HEADS, 2, HEAD_DK), pos)
    v = v.reshape(b, s, N_HEADS, HEAD_DV)
    lq = lambda_qk.astype(jnp.float32)
    lam = jnp.exp(jnp.sum(lq[0] * lq[1])) - jnp.exp(jnp.sum(lq[2] * lq[3])) + lam_init
    o = _rms(attend(q, k, v, lam), g_subln) * (1.0 - lam_init)
    y_d = o.reshape(b, s, BRANCH_W)
    ys = jnp.stack([y_a, y_b, y_c, y_d], axis=2) * jax.nn.silu(mgate.reshape(b, s, N_BRANCH, BRANCH_W))
    br = jnp.einsum('bsnw,nwd->bsnd', ys, w_branch)
    merged = jnp.sum(jax.nn.sigmoid(mrg.reshape(b, s, N_BRANCH, D_MODEL)) * br, axis=2)
    out = _rms(merged @ w_o, g_post)
    return x + gate[:, None] * out, k, v, pool_new, sconv_new, cconv_new


def setup_inputs(seed: int = 0) -> dict:
    key = jax.random.key(seed)
    ks = iter(jax.random.split(key, 32))

    def nrm(shape, sc):
        return jax.random.normal(next(ks), shape, jnp.float32) * sc

    n_pages = PAST_LEN // PAGE_SIZE
    n_used = DEC_BATCH * n_pages
    n_pool = n_used + n_used // 4
    page_table = jax.random.permutation(next(ks), n_pool)[:n_used].reshape(DEC_BATCH, n_pages).astype(jnp.int32)
    return {
        'x_prompt': nrm((BATCH, SEQ, D_MODEL), 1.0),
        'x_sample': nrm((DEC_BATCH, DEC_SEQ, D_MODEL), 1.0),
        'cache_k': nrm((DEPTH, n_pool, PAGE_SIZE, N_HEADS, 2, HEAD_DK), 1.0),
        'cache_v': nrm((DEPTH, n_pool, PAGE_SIZE, N_HEADS, HEAD_DV), 1.0),
        'page_table': page_table,
        'state_pool': nrm((DEPTH, DEC_BATCH, POOL_BUF, BRANCH_W), 1.0),
        'state_sconv': nrm((DEPTH, DEC_BATCH, SCONV_K - 1, BRANCH_W), 1.0),
        'state_cconv': nrm((DEPTH, DEC_BATCH, CCONV_K - 1, BRANCH_W), 0.5),
        'c_prompt': nrm((BATCH, D_MODEL), 1.0),
        'c_sample': nrm((DEC_BATCH, D_MODEL), 1.0),
        'w_ada': nrm((DEPTH, D_MODEL, 3 * D_MODEL), 0.5 * D_MODEL ** -0.5),
        'b_ada': nrm((DEPTH, 3 * D_MODEL), 0.01),
        'g_pre': 1.0 + nrm((DEPTH, D_MODEL), 0.05),
        'g_post': 1.0 + nrm((DEPTH, D_MODEL), 0.05),
        'w_in': nrm((DEPTH, D_MODEL, N_IN), D_MODEL ** -0.5),
        'w_pool': nrm((DEPTH, POOL_GROUPS, POOL_GC, POOL_GC), POOL_GC ** -0.5),
        'pool_scale': 1.0 + nrm((DEPTH, BRANCH_W), 0.1),
        'w_sconv': nrm((DEPTH, SCONV_K, BRANCH_W), SCONV_K ** -0.5),
        'w_cconv': nrm((DEPTH, CCONV_K, BRANCH_W), CCONV_K ** -0.5),
        'b_cconv': nrm((DEPTH, BRANCH_W), 0.01),
        'g_cnorm': 1.0 + nrm((DEPTH, BRANCH_W), 0.05),
        'b_cnorm': nrm((DEPTH, BRANCH_W), 0.01),
        'lambda_qk': nrm((DEPTH, 4, HEAD_DK), 0.1),
        'g_subln': 1.0 + nrm((DEPTH, HEAD_DV), 0.05),
        'w_branch': nrm((DEPTH, N_BRANCH, BRANCH_W, D_MODEL), BRANCH_W ** -0.5),
        'w_o': nrm((DEPTH, D_MODEL, D_MODEL), D_MODEL ** -0.5),
    }


def reference(x_prompt, x_sample, cache_k, cache_v, page_table, state_pool, state_sconv, state_cconv,
              c_prompt, c_sample, w_ada, b_ada, g_pre, g_post, w_in, w_pool, pool_scale, w_sconv,
              w_cconv, b_cconv, g_cnorm, b_cnorm, lambda_qk, g_subln, w_branch, w_o):
    bp, sp = x_prompt.shape[:2]
    db, ds = x_sample.shape[:2]
    past = page_table.shape[1] * cache_k.shape[2]
    pos_p = jnp.arange(sp)
    pos_s = past + jnp.arange(ds)
    z_pool = jnp.zeros((bp, POOL_BUF, BRANCH_W), x_prompt.dtype)
    z_sconv = jnp.zeros((bp, SCONV_K - 1, BRANCH_W), x_prompt.dtype)
    z_cconv = jnp.zeros((bp, CCONV_K - 1, BRANCH_W), x_prompt.dtype)
    xp, xs = x_prompt, x_sample
    new_p = ([], [], [], [], [])
    new_s = ([], [], [], [], [])
    for l in range(DEPTH):
        lam_init = 0.8 - 0.6 * math.exp(-0.3 * l)
        wl = (w_ada[l], b_ada[l], g_pre[l], g_post[l], w_in[l], w_pool[l], pool_scale[l], w_sconv[l],
              w_cconv[l], b_cconv[l], g_cnorm[l], b_cnorm[l], lambda_qk[l], g_subln[l], w_branch[l], w_o[l])
        xp, kp, vp, pp, scp, ccp = _layer(xp, c_prompt, pos_p, z_pool, z_sconv, z_cconv,
                                          _attend_prompt, lam_init, *wl)
        k_past = cache_k[l, page_table].reshape(db, past, N_HEADS, 2, HEAD_DK)
        v_past = cache_v[l, page_table].reshape(db, past, N_HEADS, HEAD_DV)
        attend_s = functools.partial(_attend_sample, k_past=k_past, v_past=v_past)
        xs, ks_, vs_, ps, scs, ccs = _layer(xs, c_sample, pos_s, state_pool[l], state_sconv[l], state_cconv[l],
                                            attend_s, lam_init, *wl)
        for lst, t in zip(new_p, (kp, vp, pp, scp, ccp)):
            lst.append(t)
        for lst, t in zip(new_s, (ks_, vs_, ps, scs, ccs)):
            lst.append(t)
    k_prompt, v_prompt, pool_prompt, sconv_prompt, cconv_prompt = [jnp.stack(t) for t in new_p]
    k_sample, v_sample, pool_sample, sconv_sample, cconv_sample = [jnp.stack(t) for t in new_s]
    return (xp, xs, k_prompt, v_prompt, pool_prompt, sconv_prompt, cconv_prompt,
            k_sample, v_sample, pool_sample, sconv_sample, cconv_sample)
```

```python
import functools
import math

import jax
import jax.numpy as jnp
from jax import lax
from jax.experimental import pallas as pl
from jax.experimental.pallas import tpu as pltpu

D_MODEL = 1024
N_BRANCH = 4
BRANCH_W = D_MODEL // 2
POOL_WINDOWS = (2, 4, 8, 16)
POOL_GC = BRANCH_W // len(POOL_WINDOWS)
POOL_BUF = max(POOL_WINDOWS) - 1
SCONV_K = 3
CCONV_K = 31
HEAD_DK = 64
HEAD_DV = 2 * HEAD_DK
N_HEADS = BRANCH_W // HEAD_DV
ATTN_SCALE = HEAD_DK ** -0.5
ROPE_THETA = 10000.0
EPS = 1e-6

_C_POOL = 0
_C_BG = BRANCH_W
_C_CG = 2 * BRANCH_W
_C_HB = 3 * BRANCH_W
_C_VAL = 4 * BRANCH_W
_C_GL = 5 * BRANCH_W
_C_Q = 6 * BRANCH_W
_C_K = 7 * BRANCH_W
_C_V = 8 * BRANCH_W
_C_MGATE = 9 * BRANCH_W
_C_MRG = _C_MGATE + N_BRANCH * BRANCH_W
N_IN = _C_MRG + N_BRANCH * D_MODEL

_LANES = 128
_SUBLANES = 8
_VMEM_LIMIT = 56 * 1024 * 1024

_PROMPT_TILE = 256
_ATTN_TILE = 256
_SAMPLE_GROUP = 32

_BF16 = jnp.bfloat16
_F32 = jnp.float32


def _rms(x, g):
    return x * lax.rsqrt(jnp.mean(x * x, axis=-1, keepdims=True) + EPS) * g


def _sigmoid(x):
    return 1.0 / (1.0 + jnp.exp(-x))


def _silu(x):
    return x * _sigmoid(x)


def _dot(a, b):
    return jnp.dot(a.astype(_BF16), b.astype(_BF16), preferred_element_type=_F32)


def _dot_nt(a, b):
    return lax.dot_general(a.astype(_BF16), b.astype(_BF16), (((1,), (1,)), ((), ())),
                           preferred_element_type=_F32)


def _lambda(lq_ref, lam_init):
    lq = lq_ref[...]
    a = jnp.sum(lq[0:1, :] * lq[1:2, :], axis=-1, keepdims=True)
    b = jnp.sum(lq[2:3, :] * lq[3:4, :], axis=-1, keepdims=True)
    return jnp.exp(a) - jnp.exp(b) + lam_init


def _mod_kernel(c_ref, w_ref, b_ref, o_ref):
    o_ref[...] = _dot(_silu(c_ref[...]), w_ref[...]) + b_ref[...]


def _modulation(c_all, w_ada, b_ada):
    depth = w_ada.shape[0]
    rows = c_all.shape[0]
    n_blk = 3
    return pl.pallas_call(
        _mod_kernel,
        grid=(depth, n_blk),
        in_specs=[
            pl.BlockSpec((rows, D_MODEL), lambda l, j: (0, 0)),
            pl.BlockSpec((None, D_MODEL, D_MODEL), lambda l, j: (l, 0, j)),
            pl.BlockSpec((None, 1, D_MODEL), lambda l, j: (l, 0, j)),
        ],
        out_specs=pl.BlockSpec((None, rows, D_MODEL), lambda l, j: (l, 0, j)),
        out_shape=jax.ShapeDtypeStruct((depth, rows, 3 * D_MODEL), _F32),
        compiler_params=pltpu.CompilerParams(dimension_semantics=("arbitrary", "arbitrary"),
                                             vmem_limit_bytes=_VMEM_LIMIT),
        name="adaln_mod",
    )(c_all, w_ada, b_ada.reshape(depth, 1, 3 * D_MODEL))


def _halo_steps(k_minus_1, stride):
    if stride == 1:
        return -(-k_minus_1 // _SUBLANES) * _SUBLANES
    return k_minus_1


def _mixer_kernel(x_ref, shift_ref, scale_ref, gpre_ref, win_ref, wpool_ref, pscale_ref,
                  wsc_ref, wcc_ref, bcc_ref, gcn_ref, bcn_ref, wbr_ref, cos_ref, sin_ref,
                  ppre_ref, spre_ref, cpre_ref,
                  q_ref, k_ref, v_ref, mabc_ref, gd_ref, sd_ref, pst_ref, sst_ref, cst_ref,
                  extp, exts, extc, *, tm, stride, pos_base, n_tiles):
    i = pl.program_id(1)
    hp = _halo_steps(POOL_BUF, stride) * stride
    hs = _halo_steps(SCONV_K - 1, stride) * stride
    hc = _halo_steps(CCONV_K - 1, stride) * stride

    @pl.when(i == 0)
    def _():
        extp[0:hp, :] = ppre_ref[...]
        exts[0:hs, :] = spre_ref[...]
        extc[0:hc, :] = cpre_ref[...]

    x = x_ref[...]
    h = _rms(x, gpre_ref[...]) * (1.0 + scale_ref[...]) + shift_ref[...]
    hb = h.astype(_BF16)

    def proj(c0, n):
        return jnp.dot(hb, win_ref[:, c0:c0 + n], preferred_element_type=_F32)

    def gated_branch(y, n):
        ys = y * _silu(proj(_C_MGATE + n * BRANCH_W, BRANCH_W))
        br = _dot(ys, wbr_ref[n])
        return _sigmoid(proj(_C_MRG + n * D_MODEL, D_MODEL)) * br

    u_a = proj(_C_POOL, BRANCH_W)
    extp[hp:hp + tm, :] = u_a
    row = lax.broadcasted_iota(jnp.int32, (tm, 1), 0) + i * tm
    step = row if stride == 1 else lax.shift_right_logical(row, int(math.log2(stride)))
    pos = pos_base + step
    d_parts = []
    for g, win in enumerate(POOL_WINDOWS):
        sl = slice(g * POOL_GC, (g + 1) * POOL_GC)
        cur = extp[hp:hp + tm, sl]
        acc = cur
        for d in range(1, win):
            acc = acc + extp[hp - d * stride:hp - d * stride + tm, sl]
        cnt = jnp.minimum(win, pos + 1).astype(_F32)
        d_parts.append(acc / cnt - cur)
    d_pool = jnp.concatenate(d_parts, axis=-1)
    y_a = _dot(d_pool, wpool_ref[...]) * pscale_ref[...]
    merged = gated_branch(y_a, 0)

    z_b = proj(_C_CG, BRANCH_W) * proj(_C_HB, BRANCH_W)
    exts[hs:hs + tm, :] = z_b
    conv_b = wsc_ref[SCONV_K - 1:SCONV_K, :] * z_b
    for j in range(SCONV_K - 1):
        off = hs - (SCONV_K - 1 - j) * stride
        conv_b = conv_b + wsc_ref[j:j + 1, :] * exts[off:off + tm, :]
    y_b = proj(_C_BG, BRANCH_W) * conv_b
    merged = merged + gated_branch(y_b, 1)

    z_c = proj(_C_VAL, BRANCH_W) * _sigmoid(proj(_C_GL, BRANCH_W))
    extc[hc:hc + tm, :] = z_c
    conv_parts = []
    for c0 in range(0, BRANCH_W, _LANES):
        sl = slice(c0, c0 + _LANES)
        acc = wcc_ref[CCONV_K - 1:CCONV_K, sl] * extc[hc:hc + tm, sl]
        for j in range(CCONV_K - 1):
            off = hc - (CCONV_K - 1 - j) * stride
            acc = acc + wcc_ref[j:j + 1, sl] * extc[off:off + tm, sl]
        conv_parts.append(acc)
    conv_c = jnp.concatenate(conv_parts, axis=-1) + bcc_ref[...]
    mu = jnp.mean(conv_c, axis=-1, keepdims=True)
    xc = conv_c - mu
    ln = xc * lax.rsqrt(jnp.mean(xc * xc, axis=-1, keepdims=True) + EPS) * gcn_ref[...] + bcn_ref[...]
    y_c = _silu(ln)
    merged = merged + gated_branch(y_c, 2)
    mabc_ref[...] = merged

    cos = jnp.concatenate([cos_ref[...]] * (BRANCH_W // _LANES), axis=-1)
    sin = jnp.concatenate([sin_ref[...]] * (BRANCH_W // _LANES), axis=-1)
    lane = lax.broadcasted_iota(jnp.int32, (tm, BRANCH_W), 1)
    first_half = (lane % HEAD_DK) < (HEAD_DK // 2)

    def rope(t):
        fwd = pltpu.roll(t, BRANCH_W - HEAD_DK // 2, axis=1)
        bwd = pltpu.roll(t, HEAD_DK // 2, axis=1)
        return t * cos + jnp.where(first_half, fwd, bwd) * sin

    q_ref[...] = rope(proj(_C_Q, BRANCH_W))
    k_ref[...] = rope(proj(_C_K, BRANCH_W))
    v_ref[...] = proj(_C_V, BRANCH_W)
    gd_ref[...] = _silu(proj(_C_MGATE + 3 * BRANCH_W, BRANCH_W))
    sd_ref[...] = _sigmoid(proj(_C_MRG + 3 * D_MODEL, D_MODEL))

    new_p = extp[tm:tm + hp, :]
    new_s = exts[tm:tm + hs, :]
    new_c = extc[tm:tm + hc, :]
    if n_tiles > 1:
        extp[0:hp, :] = new_p
        exts[0:hs, :] = new_s
        extc[0:hc, :] = new_c

    @pl.when(i == n_tiles - 1)
    def _():
        pst_ref[...] = new_p
        sst_ref[...] = new_s
        cst_ref[...] = new_c


def _mixer(x, shift, scale, g_pre, w_in, w_pool_bd, pool_scale, w_sconv, w_cconv, b_cconv, g_cnorm,
           b_cnorm, w_branch, cos, sin, pool_pre, sconv_pre, cconv_pre, *, tm, stride, pos_base):
    nb, rows, _ = x.shape
    n_tiles = rows // tm
    hp = _halo_steps(POOL_BUF, stride) * stride
    hs = _halo_steps(SCONV_K - 1, stride) * stride
    hc = _halo_steps(CCONV_K - 1, stride) * stride
    mod_rows = shift.shape[1]
    once = pl.Buffered(1)

    def const(shape):
        nd = len(shape)
        return pl.BlockSpec(shape, lambda b, i: (0,) * nd, pipeline_mode=once)

    def per_b(shape):
        return pl.BlockSpec((None,) + shape, lambda b, i: (b, 0, 0))

    def tiled(width):
        return pl.BlockSpec((None, tm, width), lambda b, i: (b, i, 0))

    mod_spec = per_b((1, D_MODEL)) if mod_rows == 1 else tiled(D_MODEL)
    row_vec = lambda a: a.reshape(1, -1)
    kern = functools.partial(_mixer_kernel, tm=tm, stride=stride, pos_base=pos_base, n_tiles=n_tiles)
    out_shapes = [jax.ShapeDtypeStruct((nb, rows, w), _F32)
                  for w in (BRANCH_W, BRANCH_W, BRANCH_W, D_MODEL, BRANCH_W, D_MODEL)]
    out_shapes += [jax.ShapeDtypeStruct((nb, hh, BRANCH_W), _F32) for hh in (hp, hs, hc)]
    return pl.pallas_call(
        kern,
        grid=(nb, n_tiles),
        in_specs=[
            tiled(D_MODEL), mod_spec, mod_spec, const((1, D_MODEL)),
            const((D_MODEL, N_IN)), const((BRANCH_W, BRANCH_W)), const((1, BRANCH_W)),
            const((SCONV_K, BRANCH_W)), const((CCONV_K, BRANCH_W)), const((1, BRANCH_W)),
            const((1, BRANCH_W)), const((1, BRANCH_W)), const((N_BRANCH, BRANCH_W, D_MODEL)),
            pl.BlockSpec((tm, _LANES), lambda b, i: (i, 0)), pl.BlockSpec((tm, _LANES), lambda b, i: (i, 0)),
            per_b((hp, BRANCH_W)), per_b((hs, BRANCH_W)), per_b((hc, BRANCH_W)),
        ],
        out_specs=[tiled(BRANCH_W), tiled(BRANCH_W), tiled(BRANCH_W), tiled(D_MODEL), tiled(BRANCH_W),
                   tiled(D_MODEL), per_b((hp, BRANCH_W)), per_b((hs, BRANCH_W)), per_b((hc, BRANCH_W))],
        out_shape=out_shapes,
        scratch_shapes=[pltpu.VMEM((hp + tm, BRANCH_W), _F32), pltpu.VMEM((hs + tm, BRANCH_W), _F32),
                        pltpu.VMEM((hc + tm, BRANCH_W), _F32)],
        compiler_params=pltpu.CompilerParams(dimension_semantics=("arbitrary", "arbitrary"),
                                             vmem_limit_bytes=_VMEM_LIMIT),
        name="mixer_s%d" % stride,
    )(x, shift, scale, row_vec(g_pre), w_in, w_pool_bd, row_vec(pool_scale), w_sconv, w_cconv,
      row_vec(b_cconv), row_vec(g_cnorm), row_vec(b_cnorm), w_branch, cos, sin,
      pool_pre, sconv_pre, cconv_pre)


def _attn_prompt_kernel(lq_ref, q_ref, k_ref, v_ref, o_ref, *, tq, lam_init):
    i = pl.program_id(1)
    lam = _lambda(lq_ref, lam_init)
    lane = lax.broadcasted_iota(jnp.int32, (tq, HEAD_DV), 1)
    first = lane < HEAD_DK
    causal = (lax.broadcasted_iota(jnp.int32, (tq, tq), 1) <= lax.broadcasted_iota(jnp.int32, (tq, tq), 0))

    for h in range(N_HEADS):
        hs = slice(h * HEAD_DV, (h + 1) * HEAD_DV)
        qh = q_ref[:, hs] * ATTN_SCALE
        q1 = jnp.where(first, qh, 0.0).astype(_BF16)
        q2 = jnp.where(first, 0.0, qh).astype(_BF16)

        def update(s, m, l, acc, vj):
            m_new = jnp.maximum(m, jnp.max(s, axis=-1, keepdims=True))
            p = jnp.exp(s - m_new)
            alpha = jnp.exp(m - m_new)
            l_new = alpha * l + jnp.sum(p, axis=-1, keepdims=True)
            acc_new = alpha * acc + jnp.dot(p.astype(_BF16), vj, preferred_element_type=_F32)
            return m_new, l_new, acc_new

        def tile(j, carry, masked):
            m1, l1, a1, m2, l2, a2 = carry
            r0 = pl.multiple_of(j * tq, tq)
            kj = k_ref[pl.ds(r0, tq), hs].astype(_BF16)
            vj = v_ref[pl.ds(r0, tq), hs].astype(_BF16)
            s1 = _dot_nt(q1, kj)
            s2 = _dot_nt(q2, kj)
            if masked:
                s1 = jnp.where(causal, s1, -jnp.inf)
                s2 = jnp.where(causal, s2, -jnp.inf)
            m1, l1, a1 = update(s1, m1, l1, a1, vj)
            m2, l2, a2 = update(s2, m2, l2, a2, vj)
            return m1, l1, a1, m2, l2, a2

        neg = jnp.full((tq, 1), -jnp.inf, _F32)
        zcol = jnp.zeros((tq, 1), _F32)
        zacc = jnp.zeros((tq, HEAD_DV), _F32)
        carry = (neg, zcol, zacc, neg, zcol, zacc)
        carry = lax.fori_loop(0, i, lambda j, c: tile(j, c, False), carry)
        m1, l1, a1, m2, l2, a2 = tile(i, carry, True)
        o_ref[:, hs] = a1 / l1 - lam * (a2 / l2)


def _attn_prompt(lambda_qk_l, q, k, v, lam_init):
    nb, s, _ = q.shape
    tq = _ATTN_TILE
    kern = functools.partial(_attn_prompt_kernel, tq=tq, lam_init=lam_init)
    return pl.pallas_call(
        kern,
        grid=(nb, s // tq),
        in_specs=[
            pl.BlockSpec((4, HEAD_DK), lambda b, i: (0, 0)),
            pl.BlockSpec((None, tq, BRANCH_W), lambda b, i: (b, i, 0)),
            pl.BlockSpec((None, s, BRANCH_W), lambda b, i: (b, 0, 0)),
            pl.BlockSpec((None, s, BRANCH_W), lambda b, i: (b, 0, 0)),
        ],
        out_specs=pl.BlockSpec((None, tq, BRANCH_W), lambda b, i: (b, i, 0)),
        out_shape=jax.ShapeDtypeStruct((nb, s, BRANCH_W), _F32),
        compiler_params=pltpu.CompilerParams(dimension_semantics=("arbitrary", "arbitrary"),
                                             vmem_limit_bytes=_VMEM_LIMIT),
        name="attn_prompt",
    )(lambda_qk_l, q, k, v)


def _attn_sample_kernel(pt_ref, lq_ref, q_ref, kn_ref, vn_ref, *rest, n_pages, page, t_new, lam_init):
    del pt_ref
    k_pages = rest[:n_pages]
    v_pages = rest[n_pages:2 * n_pages]
    o_ref = rest[2 * n_pages]
    s_ref, knew, vnew = rest[2 * n_pages + 1:]
    lam = _lambda(lq_ref, lam_init)
    n_rows = 2 * N_HEADS * t_new
    half = N_HEADS * t_new

    r = lax.broadcasted_iota(jnp.int32, (n_rows, BRANCH_W), 0)
    c = lax.broadcasted_iota(jnp.int32, (n_rows, BRANCH_W), 1)
    q_rep = jnp.zeros((n_rows, BRANCH_W), _F32)
    for t in range(t_new):
        q_rep = jnp.where(r % t_new == t, q_ref[t:t + 1, :] * ATTN_SCALE, q_rep)
    chunk_of_row = 2 * ((r % half) // t_new) + r // half
    wq = jnp.where(c // HEAD_DK == chunk_of_row, q_rep, 0.0).astype(_BF16)

    for p in range(n_pages):
        s_ref[:, p * page:(p + 1) * page] = _dot_nt(wq, k_pages[p][...])
    knew[...] = jnp.zeros_like(knew)
    vnew[...] = jnp.zeros_like(vnew)
    knew[0:t_new, :] = kn_ref[...]
    vnew[0:t_new, :] = vn_ref[...]
    s_new = _dot_nt(wq, knew[...])
    key = lax.broadcasted_iota(jnp.int32, (n_rows, page), 1)
    tok = lax.broadcasted_iota(jnp.int32, (n_rows, page), 0) % t_new
    s_ref[:, n_pages * page:(n_pages + 1) * page] = jnp.where(key <= tok, s_new, -jnp.inf)

    s = s_ref[...]
    m = jnp.max(s, axis=-1, keepdims=True)
    e = jnp.exp(s - m)
    pr = e / jnp.sum(e, axis=-1, keepdims=True)
    wts = (pr[0:half, :] - lam * pr[half:n_rows, :]).astype(_BF16)

    acc = jnp.dot(wts[:, n_pages * page:(n_pages + 1) * page], vnew[...].astype(_BF16),
                  preferred_element_type=_F32)
    for p in range(n_pages):
        acc = acc + jnp.dot(wts[:, p * page:(p + 1) * page], v_pages[p][...].astype(_BF16),
                            preferred_element_type=_F32)
    ar = lax.broadcasted_iota(jnp.int32, (half, BRANCH_W), 0)
    ac = lax.broadcasted_iota(jnp.int32, (half, BRANCH_W), 1)
    own = jnp.where(ar // t_new == ac // HEAD_DV, acc, 0.0)
    tot = own
    for h in range(1, N_HEADS):
        tot = tot + pltpu.roll(own, h * t_new, axis=0)
    o_ref[...] = tot[0:t_new, :]


def _attn_sample(layer, page_table, cache_k, cache_v, lambda_qk_l, q, k_new, v_new, lam_init):
    nb, t_new, _ = q.shape
    n_pages = page_table.shape[1]
    page = cache_k.shape[2]

    def page_spec(j):
        return pl.BlockSpec((None, None, page, BRANCH_W), lambda b, pt: (layer, pt[b, j], 0, 0))

    tok_spec = pl.BlockSpec((None, t_new, BRANCH_W), lambda b, pt: (b, 0, 0))
    kern = functools.partial(_attn_sample_kernel, n_pages=n_pages, page=page, t_new=t_new, lam_init=lam_init)
    n_rows = 2 * N_HEADS * t_new
    grid_spec = pltpu.PrefetchScalarGridSpec(
        num_scalar_prefetch=1,
        grid=(nb,),
        in_specs=[pl.BlockSpec((4, HEAD_DK), lambda b, pt: (0, 0)), tok_spec, tok_spec, tok_spec]
        + [page_spec(j) for j in range(n_pages)] + [page_spec(j) for j in range(n_pages)],
        out_specs=tok_spec,
        scratch_shapes=[pltpu.VMEM((n_rows, (n_pages + 1) * page), _F32),
                        pltpu.VMEM((page, BRANCH_W), _F32), pltpu.VMEM((page, BRANCH_W), _F32)],
    )
    return pl.pallas_call(
        kern,
        grid_spec=grid_spec,
        out_shape=jax.ShapeDtypeStruct((nb, t_new, BRANCH_W), _F32),
        compiler_params=pltpu.CompilerParams(dimension_semantics=("arbitrary",),
                                             vmem_limit_bytes=_VMEM_LIMIT),
        name="attn_sample",
    )(page_table, lambda_qk_l, q, k_new, v_new, *([cache_k] * n_pages), *([cache_v] * n_pages))


def _output_kernel(o_ref, gd_ref, sd_ref, mabc_ref, x_ref, gate_ref, gsub_ref, wbr_ref, wo_ref, gpost_ref,
                   y_ref, *, lam_init):
    o = o_ref[...]
    parts = [_rms(o[:, h * HEAD_DV:(h + 1) * HEAD_DV], gsub_ref[...]) * (1.0 - lam_init)
             for h in range(N_HEADS)]
    ys = jnp.concatenate(parts, axis=-1) * gd_ref[...]
    merged = mabc_ref[...] + sd_ref[...] * _dot(ys, wbr_ref[...])
    out = _rms(_dot(merged, wo_ref[...]), gpost_ref[...])
    y_ref[...] = x_ref[...] + gate_ref[...] * out


def _output(o, gd, sd, mabc, x, gate, g_subln, w_branch, w_o, g_post, *, tm, lam_init):
    nb, rows, _ = x.shape
    once = pl.Buffered(1)

    def tiled(width):
        return pl.BlockSpec((None, tm, width), lambda b, i: (b, i, 0))

    gate_spec = (pl.BlockSpec((None, 1, D_MODEL), lambda b, i: (b, 0, 0)) if gate.shape[1] == 1
                 else tiled(D_MODEL))
    kern = functools.partial(_output_kernel, lam_init=lam_init)
    return pl.pallas_call(
        kern,
        grid=(nb, rows // tm),
        in_specs=[
            tiled(BRANCH_W), tiled(BRANCH_W), tiled(D_MODEL), tiled(D_MODEL), tiled(D_MODEL), gate_spec,
            pl.BlockSpec((1, HEAD_DV), lambda b, i: (0, 0)),
            pl.BlockSpec((None, BRANCH_W, D_MODEL), lambda b, i: (N_BRANCH - 1, 0, 0), pipeline_mode=once),
            pl.BlockSpec((D_MODEL, D_MODEL), lambda b, i: (0, 0), pipeline_mode=once),
            pl.BlockSpec((1, D_MODEL), lambda b, i: (0, 0)),
        ],
        out_specs=tiled(D_MODEL),
        out_shape=jax.ShapeDtypeStruct((nb, rows, D_MODEL), _F32),
        compiler_params=pltpu.CompilerParams(dimension_semantics=("arbitrary", "arbitrary"),
                                             vmem_limit_bytes=_VMEM_LIMIT),
        name="output_proj",
    )(o, gd, sd, mabc, x, gate, g_subln.reshape(1, -1), w_branch, w_o, g_post.reshape(1, -1))


def _to_time_major(a):
    nb, t, c = a.shape
    g = nb // _SAMPLE_GROUP
    return a.reshape(g, _SAMPLE_GROUP, t, c).transpose(0, 2, 1, 3).reshape(g, t * _SAMPLE_GROUP, c)


def _from_time_major(a, t):
    g, rows, c = a.shape
    return a.reshape(g, t, _SAMPLE_GROUP, c).transpose(0, 2, 1, 3).reshape(g * _SAMPLE_GROUP, t, c)


def _rope_tables(pos):
    half = HEAD_DK // 2
    inv = jnp.power(ROPE_THETA, -jnp.arange(half, dtype=_F32) / half)
    ang = pos.astype(_F32)[:, None] * inv[None, :]
    cos, sin = jnp.cos(ang), jnp.sin(ang)
    reps = _LANES // HEAD_DK
    cos_t = jnp.tile(jnp.concatenate([cos, cos], axis=-1), (1, reps))
    sin_t = jnp.tile(jnp.concatenate([-sin, sin], axis=-1), (1, reps))
    return cos_t, sin_t


def _block_diag(w):
    g, a, b = w.shape
    out = jnp.zeros((g * a, g * b), w.dtype)
    for n in range(g):
        out = out.at[n * a:(n + 1) * a, n * b:(n + 1) * b].set(w[n])
    return out


def kernel(x_prompt, x_sample, cache_k, cache_v, page_table, state_pool, state_sconv, state_cconv,
           c_prompt, c_sample, w_ada, b_ada, g_pre, g_post, w_in, w_pool, pool_scale, w_sconv, w_cconv,
           b_cconv, g_cnorm, b_cnorm, lambda_qk, g_subln, w_branch, w_o):
    bp, sp, _ = x_prompt.shape
    db, ds, _ = x_sample.shape
    depth = w_in.shape[0]
    n_pool, page = cache_k.shape[1], cache_k.shape[2]
    past = page_table.shape[1] * page
    grp = _SAMPLE_GROUP

    mod = _modulation(jnp.concatenate([c_prompt, c_sample], axis=0), w_ada, b_ada)
    cos_p, sin_p = _rope_tables(jnp.arange(sp))
    cos_s, sin_s = _rope_tables(jnp.repeat(past + jnp.arange(ds), grp))
    cache_k2 = cache_k.reshape(depth, n_pool, page, BRANCH_W)
    cache_v2 = cache_v.reshape(depth, n_pool, page, BRANCH_W)

    hp1, hs1, hc1 = (_halo_steps(n, 1) for n in (POOL_BUF, SCONV_K - 1, CCONV_K - 1))
    zero_pre = [jnp.zeros((bp, hh, BRANCH_W), _F32) for hh in (hp1, hs1, hc1)]

    xp = x_prompt
    xs = _to_time_major(x_sample)
    outs_p = [[] for _ in range(5)]
    outs_s = [[] for _ in range(5)]
    for l in range(depth):
        lam_init = 0.8 - 0.6 * math.exp(-0.3 * l)
        w_in_l = w_in[l].astype(_BF16)
        w_br_l = w_branch[l].astype(_BF16)
        w_o_l = w_o[l].astype(_BF16)
        w_pool_l = _block_diag(w_pool[l]).astype(_BF16)
        weights = (g_pre[l], w_in_l, w_pool_l, pool_scale[l], w_sconv[l], w_cconv[l], b_cconv[l],
                   g_cnorm[l], b_cnorm[l], w_br_l)

        mod_p = mod[l, :bp]
        shift, scale, gate = (mod_p[:, None, n * D_MODEL:(n + 1) * D_MODEL] for n in range(3))
        q, k, v, mabc, gd, sd, pst, sst, cst = _mixer(
            xp, shift, scale, *weights, cos_p, sin_p, *zero_pre, tm=_PROMPT_TILE, stride=1, pos_base=0)
        o = _attn_prompt(lambda_qk[l], q, k, v, lam_init)
        xp = _output(o, gd, sd, mabc, xp, gate, g_subln[l], w_br_l, w_o_l, g_post[l],
                     tm=_PROMPT_TILE, lam_init=lam_init)
        for lst, t in zip(outs_p, (k.reshape(bp, sp, N_HEADS, 2, HEAD_DK), v.reshape(bp, sp, N_HEADS, HEAD_DV),
                                   pst[:, hp1 - POOL_BUF:], sst[:, hs1 - (SCONV_K - 1):],
                                   cst[:, hc1 - (CCONV_K - 1):])):
            lst.append(t)

        mod_s = mod[l, bp:]
        shift, scale, gate = (_to_time_major(jnp.broadcast_to(mod_s[:, None, n * D_MODEL:(n + 1) * D_MODEL],
                                                              (db, ds, D_MODEL))) for n in range(3))
        pre = [_to_time_major(st[l]) for st in (state_pool, state_sconv, state_cconv)]
        q, k, v, mabc, gd, sd, pst, sst, cst = _mixer(
            xs, shift, scale, *weights, cos_s, sin_s, *pre, tm=ds * grp, stride=grp, pos_base=past)
        q_b, k_b, v_b = (_from_time_major(t, ds) for t in (q, k, v))
        o_b = _attn_sample(l, page_table, cache_k2, cache_v2, lambda_qk[l], q_b, k_b, v_b, lam_init)
        xs = _output(_to_time_major(o_b), gd, sd, mabc, xs, gate, g_subln[l], w_br_l, w_o_l, g_post[l],
                     tm=ds * grp, lam_init=lam_init)
        for lst, t in zip(outs_s, (k_b.reshape(db, ds, N_HEADS, 2, HEAD_DK), v_b.reshape(db, ds, N_HEADS, HEAD_DV),
                                   _from_time_major(pst, POOL_BUF), _from_time_major(sst, SCONV_K - 1),
                                   _from_time_major(cst, CCONV_K - 1))):
            lst.append(t)

    y_sample = _from_time_major(xs, ds)
    stacked_p = [jnp.stack(t) for t in outs_p]
    stacked_s = [jnp.stack(t) for t in outs_s]
    return (xp, y_sample, *stacked_p, *stacked_s)
```

```python
import functools
import math

import jax
import jax.numpy as jnp
from jax import lax
from jax.experimental import pallas as pl
from jax.experimental.pallas import tpu as pltpu

D_MODEL = 1024
N_BRANCH = 4
BRANCH_W = D_MODEL // 2
POOL_WINDOWS = (2, 4, 8, 16)
POOL_GC = BRANCH_W // len(POOL_WINDOWS)
POOL_BUF = max(POOL_WINDOWS) - 1
SCONV_K = 3
CCONV_K = 31
HEAD_DK = 64
HEAD_DV = 2 * HEAD_DK
N_HEADS = BRANCH_W // HEAD_DV
ATTN_SCALE = HEAD_DK ** -0.5
ROPE_THETA = 10000.0
EPS = 1e-6

_C_POOL = 0
_C_BG = BRANCH_W
_C_CG = 2 * BRANCH_W
_C_HB = 3 * BRANCH_W
_C_VAL = 4 * BRANCH_W
_C_GL = 5 * BRANCH_W
_C_Q = 6 * BRANCH_W
_C_K = 7 * BRANCH_W
_C_V = 8 * BRANCH_W
_C_MGATE = 9 * BRANCH_W
_C_MRG = _C_MGATE + N_BRANCH * BRANCH_W
N_IN = _C_MRG + N_BRANCH * D_MODEL

_LANES = 128
_SUBLANES = 8
_VMEM_LIMIT = 56 * 1024 * 1024

_PROMPT_TILE = 256
_ATTN_TILE = 256
_SAMPLE_GROUP = 32

_BF16 = jnp.bfloat16
_F32 = jnp.float32


def _rms(x, g):
    return x * lax.rsqrt(jnp.mean(x * x, axis=-1, keepdims=True) + EPS) * g


def _sigmoid(x):
    return 1.0 / (1.0 + jnp.exp(-x))


def _silu(x):
    return x * _sigmoid(x)


def _dot(a, b):
    return jnp.dot(a.astype(_BF16), b.astype(_BF16), preferred_element_type=_F32)


def _dot_nt(a, b):
    return lax.dot_general(a.astype(_BF16), b.astype(_BF16), (((1,), (1,)), ((), ())),
                           preferred_element_type=_F32)


def _lambda(lq_ref, lam_init):
    lq = lq_ref[...]
    a = jnp.sum(lq[0:1, :] * lq[1:2, :], axis=-1, keepdims=True)
    b = jnp.sum(lq[2:3, :] * lq[3:4, :], axis=-1, keepdims=True)
    return jnp.exp(a) - jnp.exp(b) + lam_init


def _mod_kernel(c_ref, w_ref, b_ref, o_ref):
    o_ref[...] = _dot(_silu(c_ref[...]), w_ref[...]) + b_ref[...]


def _modulation(c_all, w_ada, b_ada):
    depth = w_ada.shape[0]
    rows = c_all.shape[0]
    n_blk = 3
    return pl.pallas_call(
        _mod_kernel,
        grid=(depth, n_blk),
        in_specs=[
            pl.BlockSpec((rows, D_MODEL), lambda l, j: (0, 0)),
            pl.BlockSpec((None, D_MODEL, D_MODEL), lambda l, j: (l, 0, j)),
            pl.BlockSpec((None, 1, D_MODEL), lambda l, j: (l, 0, j)),
        ],
        out_specs=pl.BlockSpec((None, rows, D_MODEL), lambda l, j: (l, 0, j)),
        out_shape=jax.ShapeDtypeStruct((depth, rows, 3 * D_MODEL), _F32),
        compiler_params=pltpu.CompilerParams(dimension_semantics=("arbitrary", "arbitrary"),
                                             vmem_limit_bytes=_VMEM_LIMIT),
        name="adaln_mod",
    )(c_all, w_ada, b_ada.reshape(depth, 1, 3 * D_MODEL))


def _halo_steps(k_minus_1, stride):
    if stride == 1:
        return -(-k_minus_1 // _SUBLANES) * _SUBLANES
    return k_minus_1


def _mixer_kernel(x_ref, shift_ref, scale_ref, gpre_ref, win_ref, wpool_ref, pscale_ref,
                  wsc_ref, wcc_ref, bcc_ref, gcn_ref, bcn_ref, wbr_ref, cos_ref, sin_ref,
                  ppre_ref, spre_ref, cpre_ref,
                  q_ref, k_ref, v_ref, mabc_ref, gd_ref, sd_ref, pst_ref, sst_ref, cst_ref,
                  extp, exts, extc, *, tm, stride, pos_base, n_tiles):
    i = pl.program_id(1)
    hp = _halo_steps(POOL_BUF, stride) * stride
    hs = _halo_steps(SCONV_K - 1, stride) * stride
    hc = _halo_steps(CCONV_K - 1, stride) * stride

    @pl.when(i == 0)
    def _():
        extp[0:hp, :] = ppre_ref[...]
        exts[0:hs, :] = spre_ref[...]
        extc[0:hc, :] = cpre_ref[...]

    x = x_ref[...]
    h = _rms(x, gpre_ref[...]) * (1.0 + scale_ref[...]) + shift_ref[...]
    hb = h.astype(_BF16)

    def proj(c0, n):
        return jnp.dot(hb, win_ref[:, c0:c0 + n], preferred_element_type=_F32)

    def gated_branch(y, n):
        ys = y * _silu(proj(_C_MGATE + n * BRANCH_W, BRANCH_W))
        br = _dot(ys, wbr_ref[n])
        return _sigmoid(proj(_C_MRG + n * D_MODEL, D_MODEL)) * br

    u_a = proj(_C_POOL, BRANCH_W)
    extp[hp:hp + tm, :] = u_a
    row = lax.broadcasted_iota(jnp.int32, (tm, 1), 0) + i * tm
    step = row if stride == 1 else lax.shift_right_logical(row, int(math.log2(stride)))
    pos = pos_base + step
    d_parts = []
    for g, win in enumerate(POOL_WINDOWS):
        sl = slice(g * POOL_GC, (g + 1) * POOL_GC)
        cur = extp[hp:hp + tm, sl]
        acc = cur
        for d in range(1, win):
            acc = acc + extp[hp - d * stride:hp - d * stride + tm, sl]
        cnt = jnp.minimum(win, pos + 1).astype(_F32)
        d_parts.append(acc / cnt - cur)
    d_pool = jnp.concatenate(d_parts, axis=-1)
    y_a = _dot(d_pool, wpool_ref[...]) * pscale_ref[...]
    merged = gated_branch(y_a, 0)

    z_b = proj(_C_CG, BRANCH_W) * proj(_C_HB, BRANCH_W)
    exts[hs:hs + tm, :] = z_b
    conv_b = wsc_ref[SCONV_K - 1:SCONV_K, :] * z_b
    for j in range(SCONV_K - 1):
        off = hs - (SCONV_K - 1 - j) * stride
        conv_b = conv_b + wsc_ref[j:j + 1, :] * exts[off:off + tm, :]
    y_b = proj(_C_BG, BRANCH_W) * conv_b
    merged = merged + gated_branch(y_b, 1)

    z_c = proj(_C_VAL, BRANCH_W) * _sigmoid(proj(_C_GL, BRANCH_W))
    extc[hc:hc + tm, :] = z_c
    conv_parts = []
    for c0 in range(0, BRANCH_W, _LANES):
        sl = slice(c0, c0 + _LANES)
        acc = wcc_ref[CCONV_K - 1:CCONV_K, sl] * extc[hc:hc + tm, sl]
        for j in range(CCONV_K - 1):
            off = hc - (CCONV_K - 1 - j) * stride
            acc = acc + wcc_ref[j:j + 1, sl] * extc[off:off + tm, sl]
        conv_parts.append(acc)
    conv_c = jnp.concatenate(conv_parts, axis=-1) + bcc_ref[...]
    mu = jnp.mean(conv_c, axis=-1, keepdims=True)
    xc = conv_c - mu
    ln = xc * lax.rsqrt(jnp.mean(xc * xc, axis=-1, keepdims=True) + EPS) * gcn_ref[...] + bcn_ref[...]
    y_c = _silu(ln)
    merged = merged + gated_branch(y_c, 2)
    mabc_ref[...] = merged

    cos = jnp.concatenate([cos_ref[...]] * (BRANCH_W // _LANES), axis=-1)
    sin = jnp.concatenate([sin_ref[...]] * (BRANCH_W // _LANES), axis=-1)
    lane = lax.broadcasted_iota(jnp.int32, (tm, BRANCH_W), 1)
    first_half = (lane % HEAD_DK) < (HEAD_DK // 2)

    def rope(t):
        fwd = pltpu.roll(t, BRANCH_W - HEAD_DK // 2, axis=1)
        bwd = pltpu.roll(t, HEAD_DK // 2, axis=1)
        return t * cos + jnp.where(first_half, fwd, bwd) * sin

    q_ref[...] = rope(proj(_C_Q, BRANCH_W))
    k_ref[...] = rope(proj(_C_K, BRANCH_W))
    v_ref[...] = proj(_C_V, BRANCH_W)
    gd_ref[...] = _silu(proj(_C_MGATE + 3 * BRANCH_W, BRANCH_W))
    sd_ref[...] = _sigmoid(proj(_C_MRG + 3 * D_MODEL, D_MODEL))

    new_p = extp[tm:tm + hp, :]
    new_s = exts[tm:tm + hs, :]
    new_c = extc[tm:tm + hc, :]
    if n_tiles > 1:
        extp[0:hp, :] = new_p
        exts[0:hs, :] = new_s
        extc[0:hc, :] = new_c

    @pl.when(i == n_tiles - 1)
    def _():
        pst_ref[...] = new_p
        sst_ref[...] = new_s
        cst_ref[...] = new_c


def _mixer(x, shift, scale, g_pre, w_in, w_pool_bd, pool_scale, w_sconv, w_cconv, b_cconv, g_cnorm,
           b_cnorm, w_branch, cos, sin, pool_pre, sconv_pre, cconv_pre, *, tm, stride, pos_base):
    nb, rows, _ = x.shape
    n_tiles = rows // tm
    hp = _halo_steps(POOL_BUF, stride) * stride
    hs = _halo_steps(SCONV_K - 1, stride) * stride
    hc = _halo_steps(CCONV_K - 1, stride) * stride
    mod_rows = shift.shape[1]
    once = pl.Buffered(1)

    def const(shape):
        nd = len(shape)
        return pl.BlockSpec(shape, lambda b, i: (0,) * nd, pipeline_mode=once)

    def per_b(shape):
        return pl.BlockSpec((None,) + shape, lambda b, i: (b, 0, 0))

    def tiled(width):
        return pl.BlockSpec((None, tm, width), lambda b, i: (b, i, 0))

    mod_spec = per_b((1, D_MODEL)) if mod_rows == 1 else tiled(D_MODEL)
    row_vec = lambda a: a.reshape(1, -1)
    kern = functools.partial(_mixer_kernel, tm=tm, stride=stride, pos_base=pos_base, n_tiles=n_tiles)
    out_shapes = [jax.ShapeDtypeStruct((nb, rows, w), _F32)
                  for w in (BRANCH_W, BRANCH_W, BRANCH_W, D_MODEL, BRANCH_W, D_MODEL)]
    out_shapes += [jax.ShapeDtypeStruct((nb, hh, BRANCH_W), _F32) for hh in (hp, hs, hc)]
    return pl.pallas_call(
        kern,
        grid=(nb, n_tiles),
        in_specs=[
            tiled(D_MODEL), mod_spec, mod_spec, const((1, D_MODEL)),
            const((D_MODEL, N_IN)), const((BRANCH_W, BRANCH_W)), const((1, BRANCH_W)),
            const((SCONV_K, BRANCH_W)), const((CCONV_K, BRANCH_W)), const((1, BRANCH_W)),
            const((1, BRANCH_W)), const((1, BRANCH_W)), const((N_BRANCH, BRANCH_W, D_MODEL)),
            pl.BlockSpec((tm, _LANES), lambda b, i: (i, 0)), pl.BlockSpec((tm, _LANES), lambda b, i: (i, 0)),
            per_b((hp, BRANCH_W)), per_b((hs, BRANCH_W)), per_b((hc, BRANCH_W)),
        ],
        out_specs=[tiled(BRANCH_W), tiled(BRANCH_W), tiled(BRANCH_W), tiled(D_MODEL), tiled(BRANCH_W),
                   tiled(D_MODEL), per_b((hp, BRANCH_W)), per_b((hs, BRANCH_W)), per_b((hc, BRANCH_W))],
        out_shape=out_shapes,
        scratch_shapes=[pltpu.VMEM((hp + tm, BRANCH_W), _F32), pltpu.VMEM((hs + tm, BRANCH_W), _F32),
                        pltpu.VMEM((hc + tm, BRANCH_W), _F32)],
        compiler_params=pltpu.CompilerParams(dimension_semantics=("arbitrary", "arbitrary"),
                                             vmem_limit_bytes=_VMEM_LIMIT),
        name="mixer_s%d" % stride,
    )(x, shift, scale, row_vec(g_pre), w_in, w_pool_bd, row_vec(pool_scale), w_sconv, w_cconv,
      row_vec(b_cconv), row_vec(g_cnorm), row_vec(b_cnorm), w_branch, cos, sin,
      pool_pre, sconv_pre, cconv_pre)


def _attn_prompt_kernel(lq_ref, q_ref, k_ref, v_ref, o_ref, *, tq, lam_init):
    i = pl.program_id(1)
    lam = _lambda(lq_ref, lam_init)
    lane = lax.broadcasted_iota(jnp.int32, (tq, HEAD_DV), 1)
    first = lane < HEAD_DK
    causal = (lax.broadcasted_iota(jnp.int32, (tq, tq), 1) <= lax.broadcasted_iota(jnp.int32, (tq, tq), 0))

    for h in range(N_HEADS):
        hs = slice(h * HEAD_DV, (h + 1) * HEAD_DV)
        qh = q_ref[:, hs] * ATTN_SCALE
        q1 = jnp.where(first, qh, 0.0).astype(_BF16)
        q2 = jnp.where(first, 0.0, qh).astype(_BF16)

        def update(s, m, l, acc, vj):
            m_new = jnp.maximum(m, jnp.max(s, axis=-1, keepdims=True))
            p = jnp.exp(s - m_new)
            alpha = jnp.exp(m - m_new)
            l_new = alpha * l + jnp.sum(p, axis=-1, keepdims=True)
            acc_new = alpha * acc + jnp.dot(p.astype(_BF16), vj, preferred_element_type=_F32)
            return m_new, l_new, acc_new

        def tile(j, carry, masked):
            m1, l1, a1, m2, l2, a2 = carry
            r0 = pl.multiple_of(j * tq, tq)
            kj = k_ref[pl.ds(r0, tq), hs].astype(_BF16)
            vj = v_ref[pl.ds(r0, tq), hs].astype(_BF16)
            s1 = _dot_nt(q1, kj)
            s2 = _dot_nt(q2, kj)
            if masked:
                s1 = jnp.where(causal, s1, -jnp.inf)
                s2 = jnp.where(causal, s2, -jnp.inf)
            m1, l1, a1 = update(s1, m1, l1, a1, vj)
            m2, l2, a2 = update(s2, m2, l2, a2, vj)
            return m1, l1, a1, m2, l2, a2

        neg = jnp.full((tq, 1), -jnp.inf, _F32)
        zcol = jnp.zeros((tq, 1), _F32)
        zacc = jnp.zeros((tq, HEAD_DV), _F32)
        carry = (neg, zcol, zacc, neg, zcol, zacc)
        carry = lax.fori_loop(0, i, lambda j, c: tile(j, c, False), carry)
        m1, l1, a1, m2, l2, a2 = tile(i, carry, True)
        o_ref[:, hs] = a1 / l1 - lam * (a2 / l2)


def _attn_prompt(lambda_qk_l, q, k, v, lam_init):
    nb, s, _ = q.shape
    tq = _ATTN_TILE
    kern = functools.partial(_attn_prompt_kernel, tq=tq, lam_init=lam_init)
    return pl.pallas_call(
        kern,
        grid=(nb, s // tq),
        in_specs=[
            pl.BlockSpec((4, HEAD_DK), lambda b, i: (0, 0)),
            pl.BlockSpec((None, tq, BRANCH_W), lambda b, i: (b, i, 0)),
            pl.BlockSpec((None, s, BRANCH_W), lambda b, i: (b, 0, 0)),
            pl.BlockSpec((None, s, BRANCH_W), lambda b, i: (b, 0, 0)),
        ],
        out_specs=pl.BlockSpec((None, tq, BRANCH_W), lambda b, i: (b, i, 0)),
        out_shape=jax.ShapeDtypeStruct((nb, s, BRANCH_W), _F32),
        compiler_params=pltpu.CompilerParams(dimension_semantics=("arbitrary", "arbitrary"),
                                             vmem_limit_bytes=_VMEM_LIMIT),
        name="attn_prompt",
    )(lambda_qk_l, q, k, v)


def _attn_sample_kernel(pt_ref, lq_ref, q_ref, kn_ref, vn_ref, *rest, n_pages, page, t_new, lam_init):
    del pt_ref
    k_pages = rest[:n_pages]
    v_pages = rest[n_pages:2 * n_pages]
    o_ref = rest[2 * n_pages]
    s_refs = rest[2 * n_pages + 1:2 * n_pages + 3]
    knew, vnew = rest[2 * n_pages + 3:]
    lam = _lambda(lq_ref, lam_init)
    rows = N_HEADS * t_new
    cols = N_HEADS * page

    @pl.when(pl.program_id(0) == 0)
    def _():
        knew[...] = jnp.zeros_like(knew)
        vnew[...] = jnp.zeros_like(vnew)

    knew[0:2 * rows, :] = kn_ref[...]
    vnew[0:rows, :] = vn_ref[...]

    rw = lax.broadcasted_iota(jnp.int32, (rows, cols), 0)
    col = lax.broadcasted_iota(jnp.int32, (rows, cols), 1)
    own_head = (col % N_HEADS) == (rw // t_new)
    causal = own_head & ((col // N_HEADS) <= (rw % t_new))
    for j in range(2):
        qj = (q_ref[j] * ATTN_SCALE).astype(_BF16)
        for p in range(n_pages):
            s = _dot_nt(qj, k_pages[p][pl.ds(j, cols, stride=2), :])
            s_refs[j][:, p * cols:(p + 1) * cols] = jnp.where(own_head, s, -jnp.inf)
        s = _dot_nt(qj, knew[pl.ds(j, cols, stride=2), :])
        s_refs[j][:, n_pages * cols:(n_pages + 1) * cols] = jnp.where(causal, s, -jnp.inf)

    def softmax(s_ref):
        s = s_ref[...]
        e = jnp.exp(s - jnp.max(s, axis=-1, keepdims=True))
        return e / jnp.sum(e, axis=-1, keepdims=True)

    wts = (softmax(s_refs[0]) - lam * softmax(s_refs[1])).astype(_BF16)
    acc = jnp.dot(wts[:, n_pages * cols:(n_pages + 1) * cols], vnew[...].astype(_BF16),
                  preferred_element_type=_F32)
    for p in range(n_pages):
        acc = acc + jnp.dot(wts[:, p * cols:(p + 1) * cols], v_pages[p][...].astype(_BF16),
                            preferred_element_type=_F32)
    for h in range(N_HEADS):
        top = acc if h == 0 else pltpu.roll(acc, rows - h * t_new, axis=0)
        o_ref[:, h * HEAD_DV:(h + 1) * HEAD_DV] = top[0:t_new, :]


def _attn_sample(layer, page_table, cache_k, cache_v, lambda_qk_l, q, k_new, v_new, lam_init):
    nb = q.shape[0]
    t_new = q.shape[2] // N_HEADS
    n_pages = page_table.shape[1]
    page = cache_v.shape[2] // N_HEADS
    rows = N_HEADS * t_new
    cols = N_HEADS * page

    def k_spec(j):
        return pl.BlockSpec((None, None, 2 * cols, HEAD_DK), lambda b, pt: (layer, pt[b, j], 0, 0))

    def v_spec(j):
        return pl.BlockSpec((None, None, cols, HEAD_DV), lambda b, pt: (layer, pt[b, j], 0, 0))

    kern = functools.partial(_attn_sample_kernel, n_pages=n_pages, page=page, t_new=t_new, lam_init=lam_init)
    grid_spec = pltpu.PrefetchScalarGridSpec(
        num_scalar_prefetch=1,
        grid=(nb,),
        in_specs=[pl.BlockSpec((4, HEAD_DK), lambda b, pt: (0, 0)),
                  pl.BlockSpec((None, 2, rows, HEAD_DK), lambda b, pt: (b, 0, 0, 0)),
                  pl.BlockSpec((None, 2 * rows, HEAD_DK), lambda b, pt: (b, 0, 0)),
                  pl.BlockSpec((None, rows, HEAD_DV), lambda b, pt: (b, 0, 0))]
        + [k_spec(j) for j in range(n_pages)] + [v_spec(j) for j in range(n_pages)],
        out_specs=pl.BlockSpec((None, t_new, BRANCH_W), lambda b, pt: (b, 0, 0)),
        scratch_shapes=[pltpu.VMEM((rows, (n_pages + 1) * cols), _F32),
                        pltpu.VMEM((rows, (n_pages + 1) * cols), _F32),
                        pltpu.VMEM((2 * cols, HEAD_DK), _F32), pltpu.VMEM((cols, HEAD_DV), _F32)],
    )
    return pl.pallas_call(
        kern,
        grid_spec=grid_spec,
        out_shape=jax.ShapeDtypeStruct((nb, t_new, BRANCH_W), _F32),
        compiler_params=pltpu.CompilerParams(dimension_semantics=("arbitrary",),
                                             vmem_limit_bytes=_VMEM_LIMIT),
        name="attn_sample",
    )(page_table, lambda_qk_l, q, k_new, v_new, *([cache_k] * n_pages), *([cache_v] * n_pages))


def _output_kernel(o_ref, gd_ref, sd_ref, mabc_ref, x_ref, gate_ref, gsub_ref, wbr_ref, wo_ref, gpost_ref,
                   y_ref, *, lam_init):
    o = o_ref[...]
    parts = [_rms(o[:, h * HEAD_DV:(h + 1) * HEAD_DV], gsub_ref[...]) * (1.0 - lam_init)
             for h in range(N_HEADS)]
    ys = jnp.concatenate(parts, axis=-1) * gd_ref[...]
    merged = mabc_ref[...] + sd_ref[...] * _dot(ys, wbr_ref[...])
    out = _rms(_dot(merged, wo_ref[...]), gpost_ref[...])
    y_ref[...] = x_ref[...] + gate_ref[...] * out


def _output(o, gd, sd, mabc, x, gate, g_subln, w_branch, w_o, g_post, *, tm, lam_init):
    nb, rows, _ = x.shape
    once = pl.Buffered(1)

    def tiled(width):
        return pl.BlockSpec((None, tm, width), lambda b, i: (b, i, 0))

    gate_spec = (pl.BlockSpec((None, 1, D_MODEL), lambda b, i: (b, 0, 0)) if gate.shape[1] == 1
                 else tiled(D_MODEL))
    kern = functools.partial(_output_kernel, lam_init=lam_init)
    return pl.pallas_call(
        kern,
        grid=(nb, rows // tm),
        in_specs=[
            tiled(BRANCH_W), tiled(BRANCH_W), tiled(D_MODEL), tiled(D_MODEL), tiled(D_MODEL), gate_spec,
            pl.BlockSpec((1, HEAD_DV), lambda b, i: (0, 0)),
            pl.BlockSpec((None, BRANCH_W, D_MODEL), lambda b, i: (N_BRANCH - 1, 0, 0), pipeline_mode=once),
            pl.BlockSpec((D_MODEL, D_MODEL), lambda b, i: (0, 0), pipeline_mode=once),
            pl.BlockSpec((1, D_MODEL), lambda b, i: (0, 0)),
        ],
        out_specs=tiled(D_MODEL),
        out_shape=jax.ShapeDtypeStruct((nb, rows, D_MODEL), _F32),
        compiler_params=pltpu.CompilerParams(dimension_semantics=("arbitrary", "arbitrary"),
                                             vmem_limit_bytes=_VMEM_LIMIT),
        name="output_proj",
    )(o, gd, sd, mabc, x, gate, g_subln.reshape(1, -1), w_branch, w_o, g_post.reshape(1, -1))


def _to_time_major(a):
    nb, t, c = a.shape
    g = nb // _SAMPLE_GROUP
    return a.reshape(g, _SAMPLE_GROUP, t, c).transpose(0, 2, 1, 3).reshape(g, t * _SAMPLE_GROUP, c)


def _from_time_major(a, t):
    g, rows, c = a.shape
    return a.reshape(g, t, _SAMPLE_GROUP, c).transpose(0, 2, 1, 3).reshape(g * _SAMPLE_GROUP, t, c)


def _rope_tables(pos):
    half = HEAD_DK // 2
    inv = jnp.power(ROPE_THETA, -jnp.arange(half, dtype=_F32) / half)
    ang = pos.astype(_F32)[:, None] * inv[None, :]
    cos, sin = jnp.cos(ang), jnp.sin(ang)
    reps = _LANES // HEAD_DK
    cos_t = jnp.tile(jnp.concatenate([cos, cos], axis=-1), (1, reps))
    sin_t = jnp.tile(jnp.concatenate([-sin, sin], axis=-1), (1, reps))
    return cos_t, sin_t


def _block_diag(w):
    g, a, b = w.shape
    out = jnp.zeros((g * a, g * b), w.dtype)
    for n in range(g):
        out = out.at[n * a:(n + 1) * a, n * b:(n + 1) * b].set(w[n])
    return out


def kernel(x_prompt, x_sample, cache_k, cache_v, page_table, state_pool, state_sconv, state_cconv,
           c_prompt, c_sample, w_ada, b_ada, g_pre, g_post, w_in, w_pool, pool_scale, w_sconv, w_cconv,
           b_cconv, g_cnorm, b_cnorm, lambda_qk, g_subln, w_branch, w_o):
    bp, sp, _ = x_prompt.shape
    db, ds, _ = x_sample.shape
    depth = w_in.shape[0]
    n_pool, page = cache_k.shape[1], cache_k.shape[2]
    past = page_table.shape[1] * page
    grp = _SAMPLE_GROUP

    mod = _modulation(jnp.concatenate([c_prompt, c_sample], axis=0), w_ada, b_ada)
    cos_p, sin_p = _rope_tables(jnp.arange(sp))
    cos_s, sin_s = _rope_tables(jnp.repeat(past + jnp.arange(ds), grp))
    cache_k2 = cache_k.reshape(depth, n_pool, page * N_HEADS * 2, HEAD_DK)
    cache_v2 = cache_v.reshape(depth, n_pool, page * N_HEADS, HEAD_DV)

    hp1, hs1, hc1 = (_halo_steps(n, 1) for n in (POOL_BUF, SCONV_K - 1, CCONV_K - 1))
    zero_pre = [jnp.zeros((bp, hh, BRANCH_W), _F32) for hh in (hp1, hs1, hc1)]

    xp = x_prompt
    xs = _to_time_major(x_sample)
    outs_p = [[] for _ in range(5)]
    outs_s = [[] for _ in range(5)]
    for l in range(depth):
        lam_init = 0.8 - 0.6 * math.exp(-0.3 * l)
        w_in_l = w_in[l].astype(_BF16)
        w_br_l = w_branch[l].astype(_BF16)
        w_o_l = w_o[l].astype(_BF16)
        w_pool_l = _block_diag(w_pool[l]).astype(_BF16)
        weights = (g_pre[l], w_in_l, w_pool_l, pool_scale[l], w_sconv[l], w_cconv[l], b_cconv[l],
                   g_cnorm[l], b_cnorm[l], w_br_l)

        mod_p = mod[l, :bp]
        shift, scale, gate = (mod_p[:, None, n * D_MODEL:(n + 1) * D_MODEL] for n in range(3))
        q, k, v, mabc, gd, sd, pst, sst, cst = _mixer(
            xp, shift, scale, *weights, cos_p, sin_p, *zero_pre, tm=_PROMPT_TILE, stride=1, pos_base=0)
        o = _attn_prompt(lambda_qk[l], q, k, v, lam_init)
        xp = _output(o, gd, sd, mabc, xp, gate, g_subln[l], w_br_l, w_o_l, g_post[l],
                     tm=_PROMPT_TILE, lam_init=lam_init)
        for lst, t in zip(outs_p, (k.reshape(bp, sp, N_HEADS, 2, HEAD_DK), v.reshape(bp, sp, N_HEADS, HEAD_DV),
                                   pst[:, hp1 - POOL_BUF:], sst[:, hs1 - (SCONV_K - 1):],
                                   cst[:, hc1 - (CCONV_K - 1):])):
            lst.append(t)

        mod_s = mod[l, bp:]
        shift, scale, gate = (_to_time_major(jnp.broadcast_to(mod_s[:, None, n * D_MODEL:(n + 1) * D_MODEL],
                                                              (db, ds, D_MODEL))) for n in range(3))
        pre = [_to_time_major(st[l]) for st in (state_pool, state_sconv, state_cconv)]
        q, k, v, mabc, gd, sd, pst, sst, cst = _mixer(
            xs, shift, scale, *weights, cos_s, sin_s, *pre, tm=ds * grp, stride=grp, pos_base=past)
        q_b, k_b, v_b = (_from_time_major(t, ds) for t in (q, k, v))
        q_hm = q_b.reshape(db, ds, N_HEADS, 2, HEAD_DK).transpose(0, 3, 2, 1, 4).reshape(db, 2, N_HEADS * ds, HEAD_DK)
        o_b = _attn_sample(l, page_table, cache_k2, cache_v2, lambda_qk[l], q_hm,
                           k_b.reshape(db, ds * N_HEADS * 2, HEAD_DK), v_b.reshape(db, ds * N_HEADS, HEAD_DV),
                           lam_init)
        xs = _output(_to_time_major(o_b), gd, sd, mabc, xs, gate, g_subln[l], w_br_l, w_o_l, g_post[l],
                     tm=ds * grp, lam_init=lam_init)
        for lst, t in zip(outs_s, (k_b.reshape(db, ds, N_HEADS, 2, HEAD_DK), v_b.reshape(db, ds, N_HEADS, HEAD_DV),
                                   _from_time_major(pst, POOL_BUF), _from_time_major(sst, SCONV_K - 1),
                                   _from_time_major(cst, CCONV_K - 1))):
            lst.append(t)

    y_sample = _from_time_major(xs, ds)
    stacked_p = [jnp.stack(t) for t in outs_p]
    stacked_s = [jnp.stack(t) for t in outs_s]
    return (xp, y_sample, *stacked_p, *stacked_s)
```

```python
import functools
import math

import jax
import jax.numpy as jnp
from jax import lax
from jax.experimental import pallas as pl
from jax.experimental.pallas import tpu as pltpu

D_MODEL = 1024
N_BRANCH = 4
BRANCH_W = D_MODEL // 2
POOL_WINDOWS = (2, 4, 8, 16)
POOL_GC = BRANCH_W // len(POOL_WINDOWS)
POOL_BUF = max(POOL_WINDOWS) - 1
SCONV_K = 3
CCONV_K = 31
HEAD_DK = 64
HEAD_DV = 2 * HEAD_DK
N_HEADS = BRANCH_W // HEAD_DV
ATTN_SCALE = HEAD_DK ** -0.5
ROPE_THETA = 10000.0
EPS = 1e-6

_C_POOL = 0
_C_BG = BRANCH_W
_C_CG = 2 * BRANCH_W
_C_HB = 3 * BRANCH_W
_C_VAL = 4 * BRANCH_W
_C_GL = 5 * BRANCH_W
_C_Q = 6 * BRANCH_W
_C_K = 7 * BRANCH_W
_C_V = 8 * BRANCH_W
_C_MGATE = 9 * BRANCH_W
_C_MRG = _C_MGATE + N_BRANCH * BRANCH_W
N_IN = _C_MRG + N_BRANCH * D_MODEL

_LANES = 128
_SUBLANES = 8
_VMEM_LIMIT = 56 * 1024 * 1024

_PROMPT_TILE = 256
_ATTN_TILE = 256
_SAMPLE_GROUP = 32

_BF16 = jnp.bfloat16
_F32 = jnp.float32
_LOG2E = math.log2(math.e)


def _rms(x, g):
    return x * lax.rsqrt(jnp.mean(x * x, axis=-1, keepdims=True) + EPS) * g


def _sigmoid(x):
    return 0.5 * jnp.tanh(0.5 * x) + 0.5


def _silu(x):
    return x * _sigmoid(x)


def _dot(a, b):
    return jnp.dot(a.astype(_BF16), b.astype(_BF16), preferred_element_type=_F32)


def _dot_nt(a, b):
    return lax.dot_general(a.astype(_BF16), b.astype(_BF16), (((1,), (1,)), ((), ())),
                           preferred_element_type=_F32)


def _lambda(lq_ref, lam_init):
    lq = lq_ref[...]
    a = jnp.sum(lq[0:1, :] * lq[1:2, :], axis=-1, keepdims=True)
    b = jnp.sum(lq[2:3, :] * lq[3:4, :], axis=-1, keepdims=True)
    return jnp.exp(a) - jnp.exp(b) + lam_init


def _mod_kernel(c_ref, w_ref, b_ref, o_ref):
    o_ref[...] = _dot(_silu(c_ref[...]), w_ref[...]) + b_ref[...]


def _modulation(c_all, w_ada, b_ada):
    depth = w_ada.shape[0]
    rows = c_all.shape[0]
    n_blk = 3
    return pl.pallas_call(
        _mod_kernel,
        grid=(depth, n_blk),
        in_specs=[
            pl.BlockSpec((rows, D_MODEL), lambda l, j: (0, 0)),
            pl.BlockSpec((None, D_MODEL, D_MODEL), lambda l, j: (l, 0, j)),
            pl.BlockSpec((None, 1, D_MODEL), lambda l, j: (l, 0, j)),
        ],
        out_specs=pl.BlockSpec((None, rows, D_MODEL), lambda l, j: (l, 0, j)),
        out_shape=jax.ShapeDtypeStruct((depth, rows, 3 * D_MODEL), _F32),
        compiler_params=pltpu.CompilerParams(dimension_semantics=("arbitrary", "arbitrary"),
                                             vmem_limit_bytes=_VMEM_LIMIT),
        name="adaln_mod",
    )(c_all, w_ada, b_ada.reshape(depth, 1, 3 * D_MODEL))


def _halo_steps(k_minus_1, stride):
    if stride == 1:
        return -(-k_minus_1 // _SUBLANES) * _SUBLANES
    return k_minus_1


def _mixer_kernel(x_ref, shift_ref, scale_ref, gpre_ref, win_ref, wpool_ref, pscale_ref,
                  wsc_ref, wcc_ref, bcc_ref, gcn_ref, bcn_ref, wbr_ref, cos_ref, sin_ref,
                  ppre_ref, spre_ref, cpre_ref,
                  q_ref, k_ref, v_ref, mabc_ref, gd_ref, sd_ref, pst_ref, sst_ref, cst_ref,
                  extp, exts, extc, hbuf, *, tm, stride, pos_base, n_tiles):
    i = pl.program_id(1)
    hp = _halo_steps(POOL_BUF, stride) * stride
    hs = _halo_steps(SCONV_K - 1, stride) * stride
    hc = _halo_steps(CCONV_K - 1, stride) * stride

    @pl.when(i == 0)
    def _():
        extp[0:hp, :] = ppre_ref[...]
        exts[0:hs, :] = spre_ref[...]
        extc[0:hc, :] = cpre_ref[...]

    x = x_ref[...]
    h = _rms(x, gpre_ref[...]) * (1.0 + scale_ref[...]) + shift_ref[...]
    hb = h.astype(_BF16)

    def proj(c0, n):
        return jnp.dot(hb, win_ref[:, c0:c0 + n], preferred_element_type=_F32)

    def gated_branch(y, n):
        ys = y * _silu(proj(_C_MGATE + n * BRANCH_W, BRANCH_W))
        br = _dot(ys, wbr_ref[n])
        return _sigmoid(proj(_C_MRG + n * D_MODEL, D_MODEL)) * br

    u_a = proj(_C_POOL, BRANCH_W)
    extp[hp:hp + tm, :] = u_a
    row = lax.broadcasted_iota(jnp.int32, (tm, 1), 0) + i * tm
    step = row if stride == 1 else lax.shift_right_logical(row, int(math.log2(stride)))
    pos = pos_base + step
    d_parts = []
    for g, win in enumerate(POOL_WINDOWS):
        sl = slice(g * POOL_GC, (g + 1) * POOL_GC)
        cur = extp[hp:hp + tm, sl]
        acc = cur
        for d in range(1, win):
            acc = acc + extp[hp - d * stride:hp - d * stride + tm, sl]
        cnt = jnp.minimum(win, pos + 1).astype(_F32)
        d_parts.append(acc / cnt - cur)
    d_pool = jnp.concatenate(d_parts, axis=-1)
    y_a = _dot(d_pool, wpool_ref[...]) * pscale_ref[...]
    merged = gated_branch(y_a, 0)

    z_b = proj(_C_CG, BRANCH_W) * proj(_C_HB, BRANCH_W)
    exts[hs:hs + tm, :] = z_b
    conv_b = wsc_ref[SCONV_K - 1:SCONV_K, :] * z_b
    for j in range(SCONV_K - 1):
        off = hs - (SCONV_K - 1 - j) * stride
        conv_b = conv_b + wsc_ref[j:j + 1, :] * exts[off:off + tm, :]
    y_b = proj(_C_BG, BRANCH_W) * conv_b
    merged = merged + gated_branch(y_b, 1)

    z_c = proj(_C_VAL, BRANCH_W) * _sigmoid(proj(_C_GL, BRANCH_W))
    extc[hc:hc + tm, :] = z_c
    groups = {}
    for j in range(CCONV_K):
        off = hc - (CCONV_K - 1 - j) * stride
        groups.setdefault(off % _SUBLANES, []).append((j, off - off % _SUBLANES))
    conv_parts = []
    for ci, c0 in enumerate(range(0, BRANCH_W, _LANES)):
        sl = slice(c0, c0 + _LANES)
        out = None
        for rem, taps in sorted(groups.items()):
            n = tm if rem == 0 else tm + _SUBLANES
            acc = None
            for j, base in taps:
                term = wcc_ref[j:j + 1, sl] * extc[base:base + n, sl]
                acc = term if acc is None else acc + term
            if rem:
                hbuf[ci, rem - 1] = acc
                acc = hbuf[ci, rem - 1, rem:rem + tm, :]
            out = acc if out is None else out + acc
        conv_parts.append(out)
    conv_c = jnp.concatenate(conv_parts, axis=-1) + bcc_ref[...]
    mu = jnp.mean(conv_c, axis=-1, keepdims=True)
    xc = conv_c - mu
    ln = xc * lax.rsqrt(jnp.mean(xc * xc, axis=-1, keepdims=True) + EPS) * gcn_ref[...] + bcn_ref[...]
    y_c = _silu(ln)
    merged = merged + gated_branch(y_c, 2)
    mabc_ref[...] = merged

    cos = jnp.concatenate([cos_ref[...]] * (BRANCH_W // _LANES), axis=-1)
    sin = jnp.concatenate([sin_ref[...]] * (BRANCH_W // _LANES), axis=-1)
    lane = lax.broadcasted_iota(jnp.int32, (tm, BRANCH_W), 1)
    first_half = (lane % HEAD_DK) < (HEAD_DK // 2)

    def rope(t):
        fwd = pltpu.roll(t, BRANCH_W - HEAD_DK // 2, axis=1)
        bwd = pltpu.roll(t, HEAD_DK // 2, axis=1)
        return t * cos + jnp.where(first_half, fwd, bwd) * sin

    q_ref[...] = rope(proj(_C_Q, BRANCH_W))
    k_ref[...] = rope(proj(_C_K, BRANCH_W))
    v_ref[...] = proj(_C_V, BRANCH_W)
    gd_ref[...] = _silu(proj(_C_MGATE + 3 * BRANCH_W, BRANCH_W))
    sd_ref[...] = _sigmoid(proj(_C_MRG + 3 * D_MODEL, D_MODEL))

    new_p = extp[tm:tm + hp, :]
    new_s = exts[tm:tm + hs, :]
    new_c = extc[tm:tm + hc, :]
    if n_tiles > 1:
        extp[0:hp, :] = new_p
        exts[0:hs, :] = new_s
        extc[0:hc, :] = new_c

    @pl.when(i == n_tiles - 1)
    def _():
        pst_ref[...] = new_p
        sst_ref[...] = new_s
        cst_ref[...] = new_c


def _mixer(x, shift, scale, g_pre, w_in, w_pool_bd, pool_scale, w_sconv, w_cconv, b_cconv, g_cnorm,
           b_cnorm, w_branch, cos, sin, pool_pre, sconv_pre, cconv_pre, *, tm, stride, pos_base):
    nb, rows, _ = x.shape
    n_tiles = rows // tm
    hp = _halo_steps(POOL_BUF, stride) * stride
    hs = _halo_steps(SCONV_K - 1, stride) * stride
    hc = _halo_steps(CCONV_K - 1, stride) * stride
    mod_rows = shift.shape[1]
    once = pl.Buffered(1)

    def const(shape):
        nd = len(shape)
        return pl.BlockSpec(shape, lambda b, i: (0,) * nd, pipeline_mode=once)

    def per_b(shape):
        return pl.BlockSpec((None,) + shape, lambda b, i: (b, 0, 0))

    def tiled(width):
        return pl.BlockSpec((None, tm, width), lambda b, i: (b, i, 0))

    mod_spec = per_b((1, D_MODEL)) if mod_rows == 1 else tiled(D_MODEL)
    row_vec = lambda a: a.reshape(1, -1)
    kern = functools.partial(_mixer_kernel, tm=tm, stride=stride, pos_base=pos_base, n_tiles=n_tiles)
    out_shapes = [jax.ShapeDtypeStruct((nb, rows, w), _F32)
                  for w in (BRANCH_W, BRANCH_W, BRANCH_W, D_MODEL, BRANCH_W, D_MODEL)]
    out_shapes += [jax.ShapeDtypeStruct((nb, hh, BRANCH_W), _F32) for hh in (hp, hs, hc)]
    return pl.pallas_call(
        kern,
        grid=(nb, n_tiles),
        in_specs=[
            tiled(D_MODEL), mod_spec, mod_spec, const((1, D_MODEL)),
            const((D_MODEL, N_IN)), const((BRANCH_W, BRANCH_W)), const((1, BRANCH_W)),
            const((SCONV_K, BRANCH_W)), const((CCONV_K, BRANCH_W)), const((1, BRANCH_W)),
            const((1, BRANCH_W)), const((1, BRANCH_W)), const((N_BRANCH, BRANCH_W, D_MODEL)),
            pl.BlockSpec((tm, _LANES), lambda b, i: (i, 0)), pl.BlockSpec((tm, _LANES), lambda b, i: (i, 0)),
            per_b((hp, BRANCH_W)), per_b((hs, BRANCH_W)), per_b((hc, BRANCH_W)),
        ],
        out_specs=[tiled(BRANCH_W), tiled(BRANCH_W), tiled(BRANCH_W), tiled(D_MODEL), tiled(BRANCH_W),
                   tiled(D_MODEL), per_b((hp, BRANCH_W)), per_b((hs, BRANCH_W)), per_b((hc, BRANCH_W))],
        out_shape=out_shapes,
        scratch_shapes=[pltpu.VMEM((hp + tm, BRANCH_W), _F32), pltpu.VMEM((hs + tm, BRANCH_W), _F32),
                        pltpu.VMEM((hc + tm, BRANCH_W), _F32),
                        pltpu.VMEM((BRANCH_W // _LANES, _SUBLANES - 1, tm + _SUBLANES, _LANES), _F32)],
        compiler_params=pltpu.CompilerParams(dimension_semantics=("arbitrary", "arbitrary"),
                                             vmem_limit_bytes=_VMEM_LIMIT),
        name="mixer_s%d" % stride,
    )(x, shift, scale, row_vec(g_pre), w_in, w_pool_bd, row_vec(pool_scale), w_sconv, w_cconv,
      row_vec(b_cconv), row_vec(g_cnorm), row_vec(b_cnorm), w_branch, cos, sin,
      pool_pre, sconv_pre, cconv_pre)


def _attn_prompt_kernel(lq_ref, q_ref, k_ref, v_ref, o_ref, vt_ref, qs_ref, acc_ref, m_ref, l_ref, *, tq,
                        lam_init):
    i = pl.program_id(1)
    n_kv = k_ref.shape[0] // tq
    lam = _lambda(lq_ref, lam_init)

    @pl.when(i == 0)
    def _():
        for h in range(N_HEADS):
            for c in range(n_kv):
                vt_ref[h, c] = v_ref[c * tq:(c + 1) * tq, h * HEAD_DV:(h + 1) * HEAD_DV].T.astype(_BF16)

    lane = lax.broadcasted_iota(jnp.int32, (tq, HEAD_DV), 1)
    first = lane < HEAD_DK
    for h in range(N_HEADS):
        qh = q_ref[:, h * HEAD_DV:(h + 1) * HEAD_DV] * (ATTN_SCALE * _LOG2E)
        qs_ref[h, 0:tq, :] = jnp.where(first, qh, 0.0).astype(_BF16)
        qs_ref[h, tq:2 * tq, :] = jnp.where(first, 0.0, qh).astype(_BF16)
    acc_ref[...] = jnp.zeros_like(acc_ref)
    m_ref[...] = jnp.full(m_ref.shape, -jnp.inf, _F32)
    l_ref[...] = jnp.zeros_like(l_ref)
    causal = (lax.broadcasted_iota(jnp.int32, (tq, 2 * tq), 0)
              <= lax.broadcasted_iota(jnp.int32, (tq, 2 * tq), 1) % tq)

    def tile(j, masked):
        r0 = pl.multiple_of(j * tq, tq)
        for h in range(N_HEADS):
            kj = k_ref[pl.ds(r0, tq), h * HEAD_DV:(h + 1) * HEAD_DV].astype(_BF16)
            st = _dot_nt(kj, qs_ref[h])
            if masked:
                st = jnp.where(causal, st, -jnp.inf)
            m = m_ref[h:h + 1, :]
            m_new = jnp.maximum(m, jnp.max(st, axis=0, keepdims=True))
            p = jnp.exp2(st - m_new)
            alpha = jnp.exp2(m - m_new)
            m_ref[h:h + 1, :] = m_new
            l_ref[h:h + 1, :] = alpha * l_ref[h:h + 1, :] + jnp.sum(p, axis=0, keepdims=True)
            acc_ref[h] = alpha * acc_ref[h] + jnp.dot(vt_ref[h, j], p.astype(_BF16),
                                                      preferred_element_type=_F32)

    def body(j, carry):
        tile(j, False)
        return carry

    lax.fori_loop(0, i, body, 0)
    tile(i, True)
    for h in range(N_HEADS):
        a = acc_ref[h] * (1.0 / l_ref[h:h + 1, :])
        o_t = a[:, 0:tq] - lam * a[:, tq:2 * tq]
        o_ref[:, h * HEAD_DV:(h + 1) * HEAD_DV] = o_t.T


def _attn_prompt(lambda_qk_l, q, k, v, lam_init):
    nb, s, _ = q.shape
    tq = _ATTN_TILE
    kern = functools.partial(_attn_prompt_kernel, tq=tq, lam_init=lam_init)
    return pl.pallas_call(
        kern,
        grid=(nb, s // tq),
        in_specs=[
            pl.BlockSpec((4, HEAD_DK), lambda b, i: (0, 0)),
            pl.BlockSpec((None, tq, BRANCH_W), lambda b, i: (b, i, 0)),
            pl.BlockSpec((None, s, BRANCH_W), lambda b, i: (b, 0, 0)),
            pl.BlockSpec((None, s, BRANCH_W), lambda b, i: (b, 0, 0)),
        ],
        out_specs=pl.BlockSpec((None, tq, BRANCH_W), lambda b, i: (b, i, 0)),
        out_shape=jax.ShapeDtypeStruct((nb, s, BRANCH_W), _F32),
        scratch_shapes=[pltpu.VMEM((N_HEADS, s // tq, HEAD_DV, tq), _BF16),
                        pltpu.VMEM((N_HEADS, 2 * tq, HEAD_DV), _BF16),
                        pltpu.VMEM((N_HEADS, HEAD_DV, 2 * tq), _F32),
                        pltpu.VMEM((_SUBLANES, 2 * tq), _F32), pltpu.VMEM((_SUBLANES, 2 * tq), _F32)],
        compiler_params=pltpu.CompilerParams(dimension_semantics=("arbitrary", "arbitrary"),
                                             vmem_limit_bytes=_VMEM_LIMIT),
        name="attn_prompt",
    )(lambda_qk_l, q, k, v)


def _attn_sample_kernel(pt_ref, lq_ref, q_ref, kn_ref, vn_ref, *rest, n_pages, page, t_new, lam_init):
    del pt_ref
    k_pages = rest[:n_pages]
    v_pages = rest[n_pages:2 * n_pages]
    o_ref = rest[2 * n_pages]
    s_ref, knew, vnew = rest[2 * n_pages + 1:]
    lam = _lambda(lq_ref, lam_init)
    n_rows = 2 * N_HEADS * t_new
    half = N_HEADS * t_new

    r = lax.broadcasted_iota(jnp.int32, (n_rows, BRANCH_W), 0)
    c = lax.broadcasted_iota(jnp.int32, (n_rows, BRANCH_W), 1)
    q_rep = jnp.zeros((n_rows, BRANCH_W), _F32)
    for t in range(t_new):
        q_rep = jnp.where(r % t_new == t, q_ref[t:t + 1, :] * ATTN_SCALE, q_rep)
    chunk_of_row = 2 * ((r % half) // t_new) + r // half
    wq = jnp.where(c // HEAD_DK == chunk_of_row, q_rep, 0.0).astype(_BF16)

    for p in range(n_pages):
        s_ref[:, p * page:(p + 1) * page] = _dot(wq, k_pages[p][...])

    @pl.when(pl.program_id(0) == 0)
    def _():
        knew[...] = jnp.zeros_like(knew)
        vnew[...] = jnp.zeros_like(vnew)

    knew[0:t_new, :] = kn_ref[...]
    vnew[0:t_new, :] = vn_ref[...]
    s_new = _dot_nt(wq, knew[...])
    key = lax.broadcasted_iota(jnp.int32, (n_rows, page), 1)
    tok = lax.broadcasted_iota(jnp.int32, (n_rows, page), 0) % t_new
    s_ref[:, n_pages * page:(n_pages + 1) * page] = jnp.where(key <= tok, s_new, -jnp.inf)

    s = s_ref[...]
    e = jnp.exp(s - jnp.max(s, axis=-1, keepdims=True))
    pr = e / jnp.sum(e, axis=-1, keepdims=True)
    wts = (pr[0:half, :] - lam * pr[half:n_rows, :]).astype(_BF16)

    acc = jnp.dot(wts[:, n_pages * page:(n_pages + 1) * page], vnew[...].astype(_BF16),
                  preferred_element_type=_F32)
    for p in range(n_pages):
        v_cat = jnp.concatenate([v_pages[p][pl.ds(h, page, stride=N_HEADS), :] for h in range(N_HEADS)],
                                axis=-1).astype(_BF16)
        acc = acc + jnp.dot(wts[:, p * page:(p + 1) * page], v_cat, preferred_element_type=_F32)
    ar = lax.broadcasted_iota(jnp.int32, (half, BRANCH_W), 0)
    ac = lax.broadcasted_iota(jnp.int32, (half, BRANCH_W), 1)
    own = jnp.where(ar // t_new == ac // HEAD_DV, acc, 0.0)
    tot = own
    for h in range(1, N_HEADS):
        tot = tot + pltpu.roll(own, h * t_new, axis=0)
    o_ref[...] = tot[0:t_new, :]


def _attn_sample(layer, page_table, cache_kt, cache_v, lambda_qk_l, q, k_new, v_new, lam_init):
    nb, t_new, _ = q.shape
    n_pages = page_table.shape[1]
    page = cache_kt.shape[3]

    def k_spec(j):
        return pl.BlockSpec((None, None, BRANCH_W, page), lambda b, pt: (layer, pt[b, j], 0, 0))

    def v_spec(j):
        return pl.BlockSpec((None, None, N_HEADS * page, HEAD_DV), lambda b, pt: (layer, pt[b, j], 0, 0))

    tok_spec = pl.BlockSpec((None, t_new, BRANCH_W), lambda b, pt: (b, 0, 0))
    kern = functools.partial(_attn_sample_kernel, n_pages=n_pages, page=page, t_new=t_new, lam_init=lam_init)
    n_rows = 2 * N_HEADS * t_new
    grid_spec = pltpu.PrefetchScalarGridSpec(
        num_scalar_prefetch=1,
        grid=(nb,),
        in_specs=[pl.BlockSpec((4, HEAD_DK), lambda b, pt: (0, 0)), tok_spec, tok_spec, tok_spec]
        + [k_spec(j) for j in range(n_pages)] + [v_spec(j) for j in range(n_pages)],
        out_specs=tok_spec,
        scratch_shapes=[pltpu.VMEM((n_rows, (n_pages + 1) * page), _F32),
                        pltpu.VMEM((page, BRANCH_W), _F32), pltpu.VMEM((page, BRANCH_W), _F32)],
    )
    return pl.pallas_call(
        kern,
        grid_spec=grid_spec,
        out_shape=jax.ShapeDtypeStruct((nb, t_new, BRANCH_W), _F32),
        compiler_params=pltpu.CompilerParams(dimension_semantics=("arbitrary",),
                                             vmem_limit_bytes=_VMEM_LIMIT),
        name="attn_sample",
    )(page_table, lambda_qk_l, q, k_new, v_new, *([cache_kt] * n_pages), *([cache_v] * n_pages))


def _output_kernel(o_ref, gd_ref, sd_ref, mabc_ref, x_ref, gate_ref, gsub_ref, wbr_ref, wo_ref, gpost_ref,
                   y_ref, *, lam_init):
    o = o_ref[...]
    parts = [_rms(o[:, h * HEAD_DV:(h + 1) * HEAD_DV], gsub_ref[...]) * (1.0 - lam_init)
             for h in range(N_HEADS)]
    ys = jnp.concatenate(parts, axis=-1) * gd_ref[...]
    merged = mabc_ref[...] + sd_ref[...] * _dot(ys, wbr_ref[...])
    out = _rms(_dot(merged, wo_ref[...]), gpost_ref[...])
    y_ref[...] = x_ref[...] + gate_ref[...] * out


def _output(o, gd, sd, mabc, x, gate, g_subln, w_branch, w_o, g_post, *, tm, lam_init):
    nb, rows, _ = x.shape
    once = pl.Buffered(1)

    def tiled(width):
        return pl.BlockSpec((None, tm, width), lambda b, i: (b, i, 0))

    gate_spec = (pl.BlockSpec((None, 1, D_MODEL), lambda b, i: (b, 0, 0)) if gate.shape[1] == 1
                 else tiled(D_MODEL))
    kern = functools.partial(_output_kernel, lam_init=lam_init)
    return pl.pallas_call(
        kern,
        grid=(nb, rows // tm),
        in_specs=[
            tiled(BRANCH_W), tiled(BRANCH_W), tiled(D_MODEL), tiled(D_MODEL), tiled(D_MODEL), gate_spec,
            pl.BlockSpec((1, HEAD_DV), lambda b, i: (0, 0)),
            pl.BlockSpec((None, BRANCH_W, D_MODEL), lambda b, i: (N_BRANCH - 1, 0, 0), pipeline_mode=once),
            pl.BlockSpec((D_MODEL, D_MODEL), lambda b, i: (0, 0), pipeline_mode=once),
            pl.BlockSpec((1, D_MODEL), lambda b, i: (0, 0)),
        ],
        out_specs=tiled(D_MODEL),
        out_shape=jax.ShapeDtypeStruct((nb, rows, D_MODEL), _F32),
        compiler_params=pltpu.CompilerParams(dimension_semantics=("arbitrary", "arbitrary"),
                                             vmem_limit_bytes=_VMEM_LIMIT),
        name="output_proj",
    )(o, gd, sd, mabc, x, gate, g_subln.reshape(1, -1), w_branch, w_o, g_post.reshape(1, -1))


def _to_time_major(a):
    nb, t, c = a.shape
    g = nb // _SAMPLE_GROUP
    return a.reshape(g, _SAMPLE_GROUP, t, c).transpose(0, 2, 1, 3).reshape(g, t * _SAMPLE_GROUP, c)


def _from_time_major(a, t):
    g, rows, c = a.shape
    return a.reshape(g, t, _SAMPLE_GROUP, c).transpose(0, 2, 1, 3).reshape(g * _SAMPLE_GROUP, t, c)


def _rope_tables(pos):
    half = HEAD_DK // 2
    inv = jnp.power(ROPE_THETA, -jnp.arange(half, dtype=_F32) / half)
    ang = pos.astype(_F32)[:, None] * inv[None, :]
    cos, sin = jnp.cos(ang), jnp.sin(ang)
    reps = _LANES // HEAD_DK
    cos_t = jnp.tile(jnp.concatenate([cos, cos], axis=-1), (1, reps))
    sin_t = jnp.tile(jnp.concatenate([-sin, sin], axis=-1), (1, reps))
    return cos_t, sin_t


def _block_diag(w):
    g, a, b = w.shape
    out = jnp.zeros((g * a, g * b), w.dtype)
    for n in range(g):
        out = out.at[n * a:(n + 1) * a, n * b:(n + 1) * b].set(w[n])
    return out


def kernel(x_prompt, x_sample, cache_k, cache_v, page_table, state_pool, state_sconv, state_cconv,
           c_prompt, c_sample, w_ada, b_ada, g_pre, g_post, w_in, w_pool, pool_scale, w_sconv, w_cconv,
           b_cconv, g_cnorm, b_cnorm, lambda_qk, g_subln, w_branch, w_o):
    bp, sp, _ = x_prompt.shape
    db, ds, _ = x_sample.shape
    depth = w_in.shape[0]
    n_pool, page = cache_k.shape[1], cache_k.shape[2]
    past = page_table.shape[1] * page
    grp = _SAMPLE_GROUP

    mod = _modulation(jnp.concatenate([c_prompt, c_sample], axis=0), w_ada, b_ada)
    cos_p, sin_p = _rope_tables(jnp.arange(sp))
    cos_s, sin_s = _rope_tables(jnp.repeat(past + jnp.arange(ds), grp))
    cache_kt = cache_k.transpose(0, 1, 3, 4, 5, 2).reshape(depth, n_pool, BRANCH_W, page)
    cache_v2 = cache_v.reshape(depth, n_pool, page * N_HEADS, HEAD_DV)

    hp1, hs1, hc1 = (_halo_steps(n, 1) for n in (POOL_BUF, SCONV_K - 1, CCONV_K - 1))
    zero_pre = [jnp.zeros((bp, hh, BRANCH_W), _F32) for hh in (hp1, hs1, hc1)]

    xp = x_prompt
    xs = _to_time_major(x_sample)
    outs_p = [[] for _ in range(5)]
    outs_s = [[] for _ in range(5)]
    for l in range(depth):
        lam_init = 0.8 - 0.6 * math.exp(-0.3 * l)
        w_in_l = w_in[l].astype(_BF16)
        w_br_l = w_branch[l].astype(_BF16)
        w_o_l = w_o[l].astype(_BF16)
        w_pool_l = _block_diag(w_pool[l]).astype(_BF16)
        weights = (g_pre[l], w_in_l, w_pool_l, pool_scale[l], w_sconv[l], w_cconv[l], b_cconv[l],
                   g_cnorm[l], b_cnorm[l], w_br_l)

        mod_p = mod[l, :bp]
        shift, scale, gate = (mod_p[:, None, n * D_MODEL:(n + 1) * D_MODEL] for n in range(3))
        q, k, v, mabc, gd, sd, pst, sst, cst = _mixer(
            xp, shift, scale, *weights, cos_p, sin_p, *zero_pre, tm=_PROMPT_TILE, stride=1, pos_base=0)
        o = _attn_prompt(lambda_qk[l], q, k, v, lam_init)
        xp = _output(o, gd, sd, mabc, xp, gate, g_subln[l], w_br_l, w_o_l, g_post[l],
                     tm=_PROMPT_TILE, lam_init=lam_init)
        for lst, t in zip(outs_p, (k.reshape(bp, sp, N_HEADS, 2, HEAD_DK), v.reshape(bp, sp, N_HEADS, HEAD_DV),
                                   pst[:, hp1 - POOL_BUF:], sst[:, hs1 - (SCONV_K - 1):],
                                   cst[:, hc1 - (CCONV_K - 1):])):
            lst.append(t)

        mod_s = mod[l, bp:]
        shift, scale, gate = (_to_time_major(jnp.broadcast_to(mod_s[:, None, n * D_MODEL:(n + 1) * D_MODEL],
                                                              (db, ds, D_MODEL))) for n in range(3))
        pre = [_to_time_major(st[l]) for st in (state_pool, state_sconv, state_cconv)]
        q, k, v, mabc, gd, sd, pst, sst, cst = _mixer(
            xs, shift, scale, *weights, cos_s, sin_s, *pre, tm=ds * grp, stride=grp, pos_base=past)
        q_b, k_b, v_b = (_from_time_major(t, ds) for t in (q, k, v))
        o_b = _attn_sample(l, page_table, cache_kt, cache_v2, lambda_qk[l], q_b, k_b, v_b, lam_init)
        xs = _output(_to_time_major(o_b), gd, sd, mabc, xs, gate, g_subln[l], w_br_l, w_o_l, g_post[l],
                     tm=ds * grp, lam_init=lam_init)
        for lst, t in zip(outs_s, (k_b.reshape(db, ds, N_HEADS, 2, HEAD_DK), v_b.reshape(db, ds, N_HEADS, HEAD_DV),
                                   _from_time_major(pst, POOL_BUF), _from_time_major(sst, SCONV_K - 1),
                                   _from_time_major(cst, CCONV_K - 1))):
            lst.append(t)

    y_sample = _from_time_major(xs, ds)
    stacked_p = [jnp.stack(t) for t in outs_p]
    stacked_s = [jnp.stack(t) for t in outs_s]
    return (xp, y_sample, *stacked_p, *stacked_s)
```

```python
import functools
import math

import jax
import jax.numpy as jnp
from jax import lax
from jax.experimental import pallas as pl
from jax.experimental.pallas import tpu as pltpu

D_MODEL = 1024
N_BRANCH = 4
BRANCH_W = D_MODEL // 2
POOL_WINDOWS = (2, 4, 8, 16)
POOL_GC = BRANCH_W // len(POOL_WINDOWS)
POOL_BUF = max(POOL_WINDOWS) - 1
SCONV_K = 3
CCONV_K = 31
HEAD_DK = 64
HEAD_DV = 2 * HEAD_DK
N_HEADS = BRANCH_W // HEAD_DV
ATTN_SCALE = HEAD_DK ** -0.5
ROPE_THETA = 10000.0
EPS = 1e-6

_C_POOL = 0
_C_BG = BRANCH_W
_C_CG = 2 * BRANCH_W
_C_HB = 3 * BRANCH_W
_C_VAL = 4 * BRANCH_W
_C_GL = 5 * BRANCH_W
_C_Q = 6 * BRANCH_W
_C_K = 7 * BRANCH_W
_C_V = 8 * BRANCH_W
_C_MGATE = 9 * BRANCH_W
_C_MRG = _C_MGATE + N_BRANCH * BRANCH_W
N_IN = _C_MRG + N_BRANCH * D_MODEL

_LANES = 128
_SUBLANES = 8
_VMEM_LIMIT = 56 * 1024 * 1024

_PROMPT_TILE = 256
_ATTN_TILE = 256
_ATTN_SUBTILES = 2
_SAMPLE_GROUP = 32

_BF16 = jnp.bfloat16
_F32 = jnp.float32
_LOG2E = math.log2(math.e)


def _rms(x, g):
    return x * lax.rsqrt(jnp.mean(x * x, axis=-1, keepdims=True) + EPS) * g


def _sigmoid(x):
    return 0.5 * jnp.tanh(0.5 * x) + 0.5


def _silu(x):
    return x * _sigmoid(x)


def _dot(a, b):
    return jnp.dot(a.astype(_BF16), b.astype(_BF16), preferred_element_type=_F32)


def _dot_nt(a, b):
    return lax.dot_general(a.astype(_BF16), b.astype(_BF16), (((1,), (1,)), ((), ())),
                           preferred_element_type=_F32)


def _lambda(lq_ref, lam_init):
    lq = lq_ref[...]
    a = jnp.sum(lq[0:1, :] * lq[1:2, :], axis=-1, keepdims=True)
    b = jnp.sum(lq[2:3, :] * lq[3:4, :], axis=-1, keepdims=True)
    return jnp.exp(a) - jnp.exp(b) + lam_init


def _mod_kernel(c_ref, w_ref, b_ref, o_ref):
    o_ref[...] = _dot(_silu(c_ref[...]), w_ref[...]) + b_ref[...]


def _modulation(c_all, w_ada, b_ada):
    depth = w_ada.shape[0]
    rows = c_all.shape[0]
    n_blk = 3
    return pl.pallas_call(
        _mod_kernel,
        grid=(depth, n_blk),
        in_specs=[
            pl.BlockSpec((rows, D_MODEL), lambda l, j: (0, 0)),
            pl.BlockSpec((None, D_MODEL, D_MODEL), lambda l, j: (l, 0, j)),
            pl.BlockSpec((None, 1, D_MODEL), lambda l, j: (l, 0, j)),
        ],
        out_specs=pl.BlockSpec((None, rows, D_MODEL), lambda l, j: (l, 0, j)),
        out_shape=jax.ShapeDtypeStruct((depth, rows, 3 * D_MODEL), _F32),
        compiler_params=pltpu.CompilerParams(dimension_semantics=("arbitrary", "arbitrary"),
                                             vmem_limit_bytes=_VMEM_LIMIT),
        name="adaln_mod",
    )(c_all, w_ada, b_ada.reshape(depth, 1, 3 * D_MODEL))


def _halo_steps(k_minus_1, stride):
    if stride == 1:
        return -(-k_minus_1 // _SUBLANES) * _SUBLANES
    return k_minus_1


def _mixer_kernel(x_ref, shift_ref, scale_ref, gpre_ref, win_ref, wpool_ref, pscale_ref,
                  wsc_ref, wcc_ref, bcc_ref, gcn_ref, bcn_ref, wbr_ref, cos_ref, sin_ref,
                  ppre_ref, spre_ref, cpre_ref,
                  q_ref, k_ref, v_ref, mabc_ref, gd_ref, sd_ref, pst_ref, sst_ref, cst_ref,
                  extp, exts, extc, hbuf, *, tm, stride, pos_base, n_tiles):
    i = pl.program_id(1)
    hp = _halo_steps(POOL_BUF, stride) * stride
    hs = _halo_steps(SCONV_K - 1, stride) * stride
    hc = _halo_steps(CCONV_K - 1, stride) * stride

    @pl.when(i == 0)
    def _():
        extp[0:hp, :] = ppre_ref[...]
        exts[0:hs, :] = spre_ref[...]
        extc[0:hc, :] = cpre_ref[...]

    x = x_ref[...]
    h = _rms(x, gpre_ref[...]) * (1.0 + scale_ref[...]) + shift_ref[...]
    hb = h.astype(_BF16)

    def proj(c0, n):
        return jnp.dot(hb, win_ref[:, c0:c0 + n], preferred_element_type=_F32)

    def gated_branch(y, n):
        ys = y * _silu(proj(_C_MGATE + n * BRANCH_W, BRANCH_W))
        br = _dot(ys, wbr_ref[n])
        return _sigmoid(proj(_C_MRG + n * D_MODEL, D_MODEL)) * br

    u_a = proj(_C_POOL, BRANCH_W)
    extp[hp:hp + tm, :] = u_a
    row = lax.broadcasted_iota(jnp.int32, (tm, 1), 0) + i * tm
    step = row if stride == 1 else lax.shift_right_logical(row, int(math.log2(stride)))
    pos = pos_base + step
    d_parts = []
    for g, win in enumerate(POOL_WINDOWS):
        sl = slice(g * POOL_GC, (g + 1) * POOL_GC)
        cur = extp[hp:hp + tm, sl]
        acc = cur
        for d in range(1, win):
            acc = acc + extp[hp - d * stride:hp - d * stride + tm, sl]
        cnt = jnp.minimum(win, pos + 1).astype(_F32)
        d_parts.append(acc / cnt - cur)
    d_pool = jnp.concatenate(d_parts, axis=-1)
    y_a = _dot(d_pool, wpool_ref[...]) * pscale_ref[...]
    merged = gated_branch(y_a, 0)

    z_b = proj(_C_CG, BRANCH_W) * proj(_C_HB, BRANCH_W)
    exts[hs:hs + tm, :] = z_b
    conv_b = wsc_ref[SCONV_K - 1:SCONV_K, :] * z_b
    for j in range(SCONV_K - 1):
        off = hs - (SCONV_K - 1 - j) * stride
        conv_b = conv_b + wsc_ref[j:j + 1, :] * exts[off:off + tm, :]
    y_b = proj(_C_BG, BRANCH_W) * conv_b
    merged = merged + gated_branch(y_b, 1)

    z_c = proj(_C_VAL, BRANCH_W) * _sigmoid(proj(_C_GL, BRANCH_W))
    extc[hc:hc + tm, :] = z_c
    groups = {}
    for j in range(CCONV_K):
        off = hc - (CCONV_K - 1 - j) * stride
        groups.setdefault(off % _SUBLANES, []).append((j, off - off % _SUBLANES))
    conv_parts = []
    for ci, c0 in enumerate(range(0, BRANCH_W, _LANES)):
        sl = slice(c0, c0 + _LANES)
        out = None
        for rem, taps in sorted(groups.items()):
            n = tm if rem == 0 else tm + _SUBLANES
            acc = None
            for j, base in taps:
                term = wcc_ref[j:j + 1, sl] * extc[base:base + n, sl]
                acc = term if acc is None else acc + term
            if rem:
                hbuf[ci, rem - 1] = acc
                acc = hbuf[ci, rem - 1, rem:rem + tm, :]
            out = acc if out is None else out + acc
        conv_parts.append(out)
    conv_c = jnp.concatenate(conv_parts, axis=-1) + bcc_ref[...]
    mu = jnp.mean(conv_c, axis=-1, keepdims=True)
    xc = conv_c - mu
    ln = xc * lax.rsqrt(jnp.mean(xc * xc, axis=-1, keepdims=True) + EPS) * gcn_ref[...] + bcn_ref[...]
    y_c = _silu(ln)
    merged = merged + gated_branch(y_c, 2)
    mabc_ref[...] = merged.astype(mabc_ref.dtype)

    cos = jnp.concatenate([cos_ref[...]] * (BRANCH_W // _LANES), axis=-1)
    sin = jnp.concatenate([sin_ref[...]] * (BRANCH_W // _LANES), axis=-1)
    lane = lax.broadcasted_iota(jnp.int32, (tm, BRANCH_W), 1)
    first_half = (lane % HEAD_DK) < (HEAD_DK // 2)

    def rope(t):
        fwd = pltpu.roll(t, BRANCH_W - HEAD_DK // 2, axis=1)
        bwd = pltpu.roll(t, HEAD_DK // 2, axis=1)
        return t * cos + jnp.where(first_half, fwd, bwd) * sin

    q_ref[...] = (rope(proj(_C_Q, BRANCH_W)) * (ATTN_SCALE * _LOG2E)).astype(q_ref.dtype)
    k_ref[...] = rope(proj(_C_K, BRANCH_W))
    v_ref[...] = proj(_C_V, BRANCH_W)
    gd_ref[...] = _silu(proj(_C_MGATE + 3 * BRANCH_W, BRANCH_W)).astype(gd_ref.dtype)
    sd_ref[...] = _sigmoid(proj(_C_MRG + 3 * D_MODEL, D_MODEL)).astype(sd_ref.dtype)

    new_p = extp[tm:tm + hp, :]
    new_s = exts[tm:tm + hs, :]
    new_c = extc[tm:tm + hc, :]
    if n_tiles > 1:
        extp[0:hp, :] = new_p
        exts[0:hs, :] = new_s
        extc[0:hc, :] = new_c

    @pl.when(i == n_tiles - 1)
    def _():
        pst_ref[...] = new_p
        sst_ref[...] = new_s
        cst_ref[...] = new_c


def _mixer(x, shift, scale, g_pre, w_in, w_pool_bd, pool_scale, w_sconv, w_cconv, b_cconv, g_cnorm,
           b_cnorm, w_branch, cos, sin, pool_pre, sconv_pre, cconv_pre, *, layer, tm, stride, pos_base):
    nb, rows, _ = x.shape
    n_tiles = rows // tm
    hp = _halo_steps(POOL_BUF, stride) * stride
    hs = _halo_steps(SCONV_K - 1, stride) * stride
    hc = _halo_steps(CCONV_K - 1, stride) * stride
    mod_rows = shift.shape[1]
    once = pl.Buffered(1)

    def const(shape):
        nd = len(shape)
        return pl.BlockSpec(shape, lambda b, i: (0,) * nd, pipeline_mode=once)

    def per_layer(shape):
        nd = len(shape)
        return pl.BlockSpec((None,) + shape, lambda b, i: (layer,) + (0,) * nd, pipeline_mode=once)

    def per_b(shape):
        return pl.BlockSpec((None,) + shape, lambda b, i: (b, 0, 0))

    def tiled(width):
        return pl.BlockSpec((None, tm, width), lambda b, i: (b, i, 0))

    mod_spec = per_b((1, D_MODEL)) if mod_rows == 1 else tiled(D_MODEL)
    row_vec = lambda a: a.reshape(1, -1)
    kern = functools.partial(_mixer_kernel, tm=tm, stride=stride, pos_base=pos_base, n_tiles=n_tiles)
    out_shapes = [jax.ShapeDtypeStruct((nb, rows, w), dt)
                  for w, dt in ((BRANCH_W, _BF16), (BRANCH_W, _F32), (BRANCH_W, _F32), (D_MODEL, _BF16),
                                (BRANCH_W, _BF16), (D_MODEL, _BF16))]
    out_shapes += [jax.ShapeDtypeStruct((nb, hh, BRANCH_W), _F32) for hh in (hp, hs, hc)]
    return pl.pallas_call(
        kern,
        grid=(nb, n_tiles),
        in_specs=[
            tiled(D_MODEL), mod_spec, mod_spec, const((1, D_MODEL)),
            per_layer((D_MODEL, N_IN)), const((BRANCH_W, BRANCH_W)), const((1, BRANCH_W)),
            const((SCONV_K, BRANCH_W)), const((CCONV_K, BRANCH_W)), const((1, BRANCH_W)),
            const((1, BRANCH_W)), const((1, BRANCH_W)), per_layer((N_BRANCH, BRANCH_W, D_MODEL)),
            pl.BlockSpec((tm, _LANES), lambda b, i: (i, 0)), pl.BlockSpec((tm, _LANES), lambda b, i: (i, 0)),
            per_b((hp, BRANCH_W)), per_b((hs, BRANCH_W)), per_b((hc, BRANCH_W)),
        ],
        out_specs=[tiled(BRANCH_W), tiled(BRANCH_W), tiled(BRANCH_W), tiled(D_MODEL), tiled(BRANCH_W),
                   tiled(D_MODEL), per_b((hp, BRANCH_W)), per_b((hs, BRANCH_W)), per_b((hc, BRANCH_W))],
        out_shape=out_shapes,
        scratch_shapes=[pltpu.VMEM((hp + tm, BRANCH_W), _F32), pltpu.VMEM((hs + tm, BRANCH_W), _F32),
                        pltpu.VMEM((hc + tm, BRANCH_W), _F32),
                        pltpu.VMEM((BRANCH_W // _LANES, _SUBLANES - 1, tm + _SUBLANES, _LANES), _F32)],
        compiler_params=pltpu.CompilerParams(dimension_semantics=("arbitrary", "arbitrary"),
                                             vmem_limit_bytes=_VMEM_LIMIT),
        name="mixer_s%d" % stride,
    )(x, shift, scale, row_vec(g_pre), w_in, w_pool_bd, row_vec(pool_scale), w_sconv, w_cconv,
      row_vec(b_cconv), row_vec(g_cnorm), row_vec(b_cnorm), w_branch, cos, sin,
      pool_pre, sconv_pre, cconv_pre)


def _attn_prompt_kernel(lq_ref, q_ref, k_ref, v_ref, o_ref, vt_ref, qs_ref, acc_ref, m_ref, l_ref, st_ref, *,
                        tq, n_sub, lam_init):
    i = pl.program_id(1)
    n_kv = k_ref.shape[0] // tq
    lam = _lambda(lq_ref, lam_init)

    @pl.when(i == 0)
    def _():
        for h in range(N_HEADS):
            for c in range(n_kv):
                vt_ref[h, c] = v_ref[c * tq:(c + 1) * tq, h * HEAD_DV:(h + 1) * HEAD_DV].T.astype(_BF16)

    lane = lax.broadcasted_iota(jnp.int32, (tq, HEAD_DV), 1)
    first = lane < HEAD_DK
    for s in range(n_sub):
        for h in range(N_HEADS):
            qh = q_ref[s * tq:(s + 1) * tq, h * HEAD_DV:(h + 1) * HEAD_DV].astype(_F32)
            qs_ref[s * N_HEADS + h, 0:tq, :] = jnp.where(first, qh, 0.0).astype(_BF16)
            qs_ref[s * N_HEADS + h, tq:2 * tq, :] = jnp.where(first, 0.0, qh).astype(_BF16)
    acc_ref[...] = jnp.zeros_like(acc_ref)
    m_ref[...] = jnp.full(m_ref.shape, -jnp.inf, _F32)
    l_ref[...] = jnp.zeros_like(l_ref)
    causal = (lax.broadcasted_iota(jnp.int32, (tq, 2 * tq), 0)
              <= lax.broadcasted_iota(jnp.int32, (tq, 2 * tq), 1) % tq)

    def tile(j, subs):
        r0 = pl.multiple_of(j * tq, tq)
        for h in range(N_HEADS):
            kj = k_ref[pl.ds(r0, tq), h * HEAD_DV:(h + 1) * HEAD_DV].astype(_BF16)
            for s, _ in subs:
                st_ref[s * N_HEADS + h] = _dot_nt(kj, qs_ref[s * N_HEADS + h])
        for h in range(N_HEADS):
            vtj = vt_ref[h, j]
            for s, masked in subs:
                c = s * N_HEADS + h
                st = st_ref[c]
                if masked:
                    st = jnp.where(causal, st, -jnp.inf)
                m = m_ref[c:c + 1, :]
                m_new = jnp.maximum(m, jnp.max(st, axis=0, keepdims=True))
                p = jnp.exp2(st - m_new)
                alpha = jnp.exp2(m - m_new)
                m_ref[c:c + 1, :] = m_new
                l_ref[c:c + 1, :] = alpha * l_ref[c:c + 1, :] + jnp.sum(p, axis=0, keepdims=True)
                acc_ref[c] = alpha * acc_ref[c] + jnp.dot(vtj, p.astype(_BF16), preferred_element_type=_F32)

    def body(j, carry):
        tile(j, [(s, False) for s in range(n_sub)])
        return carry

    lax.fori_loop(0, n_sub * i, body, 0)
    for t in range(n_sub):
        tile(n_sub * i + t, [(t, True)] + [(s, False) for s in range(t + 1, n_sub)])
    for s in range(n_sub):
        for h in range(N_HEADS):
            c = s * N_HEADS + h
            a = acc_ref[c] * (1.0 / l_ref[c:c + 1, :])
            o_t = a[:, 0:tq] - lam * a[:, tq:2 * tq]
            o_ref[s * tq:(s + 1) * tq, h * HEAD_DV:(h + 1) * HEAD_DV] = o_t.T.astype(o_ref.dtype)


def _attn_prompt(lambda_qk_l, q, k, v, lam_init):
    nb, s, _ = q.shape
    tq = _ATTN_TILE
    n_sub = _ATTN_SUBTILES
    n_chain = n_sub * N_HEADS
    kern = functools.partial(_attn_prompt_kernel, tq=tq, n_sub=n_sub, lam_init=lam_init)
    return pl.pallas_call(
        kern,
        grid=(nb, s // (n_sub * tq)),
        in_specs=[
            pl.BlockSpec((4, HEAD_DK), lambda b, i: (0, 0)),
            pl.BlockSpec((None, n_sub * tq, BRANCH_W), lambda b, i: (b, i, 0)),
            pl.BlockSpec((None, s, BRANCH_W), lambda b, i: (b, 0, 0)),
            pl.BlockSpec((None, s, BRANCH_W), lambda b, i: (b, 0, 0)),
        ],
        out_specs=pl.BlockSpec((None, n_sub * tq, BRANCH_W), lambda b, i: (b, i, 0)),
        out_shape=jax.ShapeDtypeStruct((nb, s, BRANCH_W), _BF16),
        scratch_shapes=[pltpu.VMEM((N_HEADS, s // tq, HEAD_DV, tq), _BF16),
                        pltpu.VMEM((n_chain, 2 * tq, HEAD_DV), _BF16),
                        pltpu.VMEM((n_chain, HEAD_DV, 2 * tq), _F32),
                        pltpu.VMEM((n_chain, 2 * tq), _F32), pltpu.VMEM((n_chain, 2 * tq), _F32),
                        pltpu.VMEM((n_chain, tq, 2 * tq), _F32)],
        compiler_params=pltpu.CompilerParams(dimension_semantics=("arbitrary", "arbitrary"),
                                             vmem_limit_bytes=_VMEM_LIMIT),
        name="attn_prompt",
    )(lambda_qk_l, q, k, v)


def _attn_sample_kernel(pt_ref, lq_ref, q_ref, kn_ref, vn_ref, *rest, n_pages, page, t_new, lam_init):
    del pt_ref
    k_pages = rest[:n_pages]
    v_pages = rest[n_pages:2 * n_pages]
    o_ref = rest[2 * n_pages]
    s_ref, knew, vnew = rest[2 * n_pages + 1:]
    lam = _lambda(lq_ref, lam_init)
    n_rows = 2 * N_HEADS * t_new
    half = N_HEADS * t_new

    r = lax.broadcasted_iota(jnp.int32, (n_rows, BRANCH_W), 0)
    c = lax.broadcasted_iota(jnp.int32, (n_rows, BRANCH_W), 1)
    q_rep = jnp.zeros((n_rows, BRANCH_W), _F32)
    for t in range(t_new):
        q_rep = jnp.where(r % t_new == t, q_ref[t:t + 1, :], q_rep)
    chunk_of_row = 2 * ((r % half) // t_new) + r // half
    wq = jnp.where(c // HEAD_DK == chunk_of_row, q_rep, 0.0).astype(_BF16)

    for p in range(n_pages):
        s_ref[:, p * page:(p + 1) * page] = _dot(wq, k_pages[p][...])

    @pl.when(pl.program_id(0) == 0)
    def _():
        knew[...] = jnp.zeros_like(knew)
        vnew[...] = jnp.zeros_like(vnew)

    knew[0:t_new, :] = kn_ref[...]
    vnew[0:t_new, :] = vn_ref[...]
    s_new = _dot_nt(wq, knew[...])
    key = lax.broadcasted_iota(jnp.int32, (n_rows, page), 1)
    tok = lax.broadcasted_iota(jnp.int32, (n_rows, page), 0) % t_new
    s_ref[:, n_pages * page:(n_pages + 1) * page] = jnp.where(key <= tok, s_new, -jnp.inf)

    s = s_ref[...]
    e = jnp.exp2(s - jnp.max(s, axis=-1, keepdims=True))
    pr = e / jnp.sum(e, axis=-1, keepdims=True)
    wts = (pr[0:half, :] - lam * pr[half:n_rows, :]).astype(_BF16)

    acc = jnp.dot(wts[:, n_pages * page:(n_pages + 1) * page], vnew[...].astype(_BF16),
                  preferred_element_type=_F32)
    for p in range(n_pages):
        v_cat = jnp.concatenate([v_pages[p][pl.ds(h, page, stride=N_HEADS), :] for h in range(N_HEADS)],
                                axis=-1).astype(_BF16)
        acc = acc + jnp.dot(wts[:, p * page:(p + 1) * page], v_cat, preferred_element_type=_F32)
    ar = lax.broadcasted_iota(jnp.int32, (half, BRANCH_W), 0)
    ac = lax.broadcasted_iota(jnp.int32, (half, BRANCH_W), 1)
    own = jnp.where(ar // t_new == ac // HEAD_DV, acc, 0.0)
    tot = own
    for h in range(1, N_HEADS):
        tot = tot + pltpu.roll(own, h * t_new, axis=0)
    o_ref[...] = tot[0:t_new, :]


def _attn_sample(layer, page_table, cache_kt, cache_v, lambda_qk_l, q, k_new, v_new, lam_init):
    nb, t_new, _ = q.shape
    n_pages = page_table.shape[1]
    page = cache_kt.shape[3]

    def k_spec(j):
        return pl.BlockSpec((None, None, BRANCH_W, page), lambda b, pt: (layer, pt[b, j], 0, 0))

    def v_spec(j):
        return pl.BlockSpec((None, None, N_HEADS * page, HEAD_DV), lambda b, pt: (layer, pt[b, j], 0, 0))

    tok_spec = pl.BlockSpec((None, t_new, BRANCH_W), lambda b, pt: (b, 0, 0))
    kern = functools.partial(_attn_sample_kernel, n_pages=n_pages, page=page, t_new=t_new, lam_init=lam_init)
    n_rows = 2 * N_HEADS * t_new
    grid_spec = pltpu.PrefetchScalarGridSpec(
        num_scalar_prefetch=1,
        grid=(nb,),
        in_specs=[pl.BlockSpec((4, HEAD_DK), lambda b, pt: (0, 0)), tok_spec, tok_spec, tok_spec]
        + [k_spec(j) for j in range(n_pages)] + [v_spec(j) for j in range(n_pages)],
        out_specs=tok_spec,
        scratch_shapes=[pltpu.VMEM((n_rows, (n_pages + 1) * page), _F32),
                        pltpu.VMEM((page, BRANCH_W), _F32), pltpu.VMEM((page, BRANCH_W), _F32)],
    )
    return pl.pallas_call(
        kern,
        grid_spec=grid_spec,
        out_shape=jax.ShapeDtypeStruct((nb, t_new, BRANCH_W), _F32),
        compiler_params=pltpu.CompilerParams(dimension_semantics=("arbitrary",),
                                             vmem_limit_bytes=_VMEM_LIMIT),
        name="attn_sample",
    )(page_table, lambda_qk_l, q, k_new, v_new, *([cache_kt] * n_pages), *([cache_v] * n_pages))


def _output_kernel(o_ref, gd_ref, sd_ref, mabc_ref, x_ref, gate_ref, gsub_ref, wbr_ref, wo_ref, gpost_ref,
                   y_ref, *, lam_init):
    o = o_ref[...].astype(_F32)
    parts = [_rms(o[:, h * HEAD_DV:(h + 1) * HEAD_DV], gsub_ref[...]) * (1.0 - lam_init)
             for h in range(N_HEADS)]
    ys = jnp.concatenate(parts, axis=-1) * gd_ref[...].astype(_F32)
    merged = mabc_ref[...].astype(_F32) + sd_ref[...].astype(_F32) * _dot(ys, wbr_ref[...])
    out = _rms(_dot(merged, wo_ref[...]), gpost_ref[...])
    y_ref[...] = x_ref[...] + gate_ref[...] * out


def _output(o, gd, sd, mabc, x, gate, g_subln, w_branch, w_o, g_post, *, layer, tm, lam_init):
    nb, rows, _ = x.shape
    once = pl.Buffered(1)

    def tiled(width):
        return pl.BlockSpec((None, tm, width), lambda b, i: (b, i, 0))

    gate_spec = (pl.BlockSpec((None, 1, D_MODEL), lambda b, i: (b, 0, 0)) if gate.shape[1] == 1
                 else tiled(D_MODEL))
    kern = functools.partial(_output_kernel, lam_init=lam_init)
    return pl.pallas_call(
        kern,
        grid=(nb, rows // tm),
        in_specs=[
            tiled(BRANCH_W), tiled(BRANCH_W), tiled(D_MODEL), tiled(D_MODEL), tiled(D_MODEL), gate_spec,
            pl.BlockSpec((1, HEAD_DV), lambda b, i: (0, 0)),
            pl.BlockSpec((None, None, BRANCH_W, D_MODEL), lambda b, i: (layer, N_BRANCH - 1, 0, 0),
                         pipeline_mode=once),
            pl.BlockSpec((None, D_MODEL, D_MODEL), lambda b, i: (layer, 0, 0), pipeline_mode=once),
            pl.BlockSpec((1, D_MODEL), lambda b, i: (0, 0)),
        ],
        out_specs=tiled(D_MODEL),
        out_shape=jax.ShapeDtypeStruct((nb, rows, D_MODEL), _F32),
        compiler_params=pltpu.CompilerParams(dimension_semantics=("arbitrary", "arbitrary"),
                                             vmem_limit_bytes=_VMEM_LIMIT),
        name="output_proj",
    )(o, gd, sd, mabc, x, gate, g_subln.reshape(1, -1), w_branch, w_o, g_post.reshape(1, -1))


def _to_time_major(a):
    nb, t, c = a.shape
    g = nb // _SAMPLE_GROUP
    return a.reshape(g, _SAMPLE_GROUP, t, c).transpose(0, 2, 1, 3).reshape(g, t * _SAMPLE_GROUP, c)


def _from_time_major(a, t):
    g, rows, c = a.shape
    return a.reshape(g, t, _SAMPLE_GROUP, c).transpose(0, 2, 1, 3).reshape(g * _SAMPLE_GROUP, t, c)


def _rope_tables(pos):
    half = HEAD_DK // 2
    inv = jnp.power(ROPE_THETA, -jnp.arange(half, dtype=_F32) / half)
    ang = pos.astype(_F32)[:, None] * inv[None, :]
    cos, sin = jnp.cos(ang), jnp.sin(ang)
    reps = _LANES // HEAD_DK
    cos_t = jnp.tile(jnp.concatenate([cos, cos], axis=-1), (1, reps))
    sin_t = jnp.tile(jnp.concatenate([-sin, sin], axis=-1), (1, reps))
    return cos_t, sin_t


def _block_diag(w):
    g, a, b = w.shape
    out = jnp.zeros((g * a, g * b), w.dtype)
    for n in range(g):
        out = out.at[n * a:(n + 1) * a, n * b:(n + 1) * b].set(w[n])
    return out


def kernel(x_prompt, x_sample, cache_k, cache_v, page_table, state_pool, state_sconv, state_cconv,
           c_prompt, c_sample, w_ada, b_ada, g_pre, g_post, w_in, w_pool, pool_scale, w_sconv, w_cconv,
           b_cconv, g_cnorm, b_cnorm, lambda_qk, g_subln, w_branch, w_o):
    bp, sp, _ = x_prompt.shape
    db, ds, _ = x_sample.shape
    depth = w_in.shape[0]
    n_pool, page = cache_k.shape[1], cache_k.shape[2]
    past = page_table.shape[1] * page
    grp = _SAMPLE_GROUP

    mod = _modulation(jnp.concatenate([c_prompt, c_sample], axis=0), w_ada, b_ada)
    cos_p, sin_p = _rope_tables(jnp.arange(sp))
    cos_s, sin_s = _rope_tables(jnp.repeat(past + jnp.arange(ds), grp))
    cache_kt = cache_k.transpose(0, 1, 3, 4, 5, 2).reshape(depth, n_pool, BRANCH_W, page)
    cache_v2 = cache_v.reshape(depth, n_pool, page * N_HEADS, HEAD_DV)

    hp1, hs1, hc1 = (_halo_steps(n, 1) for n in (POOL_BUF, SCONV_K - 1, CCONV_K - 1))
    zero_pre = [jnp.zeros((bp, hh, BRANCH_W), _F32) for hh in (hp1, hs1, hc1)]

    w_in_b, w_br_b, w_o_b = (w.astype(_BF16) for w in (w_in, w_branch, w_o))
    xp = x_prompt
    xs = _to_time_major(x_sample)
    outs_p = [[] for _ in range(5)]
    outs_s = [[] for _ in range(5)]
    for l in range(depth):
        lam_init = 0.8 - 0.6 * math.exp(-0.3 * l)
        w_pool_l = _block_diag(w_pool[l]).astype(_BF16)
        weights = (g_pre[l], w_in_b, w_pool_l, pool_scale[l], w_sconv[l], w_cconv[l], b_cconv[l],
                   g_cnorm[l], b_cnorm[l], w_br_b)

        mod_p = mod[l, :bp]
        shift, scale, gate = (mod_p[:, None, n * D_MODEL:(n + 1) * D_MODEL] for n in range(3))
        q, k, v, mabc, gd, sd, pst, sst, cst = _mixer(
            xp, shift, scale, *weights, cos_p, sin_p, *zero_pre, layer=l, tm=_PROMPT_TILE, stride=1, pos_base=0)
        o = _attn_prompt(lambda_qk[l], q, k, v, lam_init)
        xp = _output(o, gd, sd, mabc, xp, gate, g_subln[l], w_br_b, w_o_b, g_post[l],
                     layer=l, tm=_PROMPT_TILE, lam_init=lam_init)
        for lst, t in zip(outs_p, (k.reshape(bp, sp, N_HEADS, 2, HEAD_DK), v.reshape(bp, sp, N_HEADS, HEAD_DV),
                                   pst[:, hp1 - POOL_BUF:], sst[:, hs1 - (SCONV_K - 1):],
                                   cst[:, hc1 - (CCONV_K - 1):])):
            lst.append(t)

        mod_s = mod[l, bp:]
        shift, scale, gate = (_to_time_major(jnp.broadcast_to(mod_s[:, None, n * D_MODEL:(n + 1) * D_MODEL],
                                                              (db, ds, D_MODEL))) for n in range(3))
        pre = [_to_time_major(st[l]) for st in (state_pool, state_sconv, state_cconv)]
        q, k, v, mabc, gd, sd, pst, sst, cst = _mixer(
            xs, shift, scale, *weights, cos_s, sin_s, *pre, layer=l, tm=ds * grp, stride=grp, pos_base=past)
        q_b, k_b, v_b = (_from_time_major(t, ds) for t in (q.astype(_F32), k, v))
        o_b = _attn_sample(l, page_table, cache_kt, cache_v2, lambda_qk[l], q_b, k_b, v_b, lam_init)
        xs = _output(_to_time_major(o_b.astype(_BF16)), gd, sd, mabc, xs, gate, g_subln[l], w_br_b, w_o_b, g_post[l],
                     layer=l, tm=ds * grp, lam_init=lam_init)
        for lst, t in zip(outs_s, (k_b.reshape(db, ds, N_HEADS, 2, HEAD_DK), v_b.reshape(db, ds, N_HEADS, HEAD_DV),
                                   _from_time_major(pst, POOL_BUF), _from_time_major(sst, SCONV_K - 1),
                                   _from_time_major(cst, CCONV_K - 1))):
            lst.append(t)

    y_sample = _from_time_major(xs, ds)
    stacked_p = [jnp.stack(t) for t in outs_p]
    stacked_s = [jnp.stack(t) for t in outs_s]
    return (xp, y_sample, *stacked_p, *stacked_s)
```

```python
import functools
import math

import jax
import jax.numpy as jnp
from jax import lax
from jax.experimental import pallas as pl
from jax.experimental.pallas import tpu as pltpu

D_MODEL = 1024
N_BRANCH = 4
BRANCH_W = D_MODEL // 2
POOL_WINDOWS = (2, 4, 8, 16)
POOL_GC = BRANCH_W // len(POOL_WINDOWS)
POOL_BUF = max(POOL_WINDOWS) - 1
SCONV_K = 3
CCONV_K = 31
HEAD_DK = 64
HEAD_DV = 2 * HEAD_DK
N_HEADS = BRANCH_W // HEAD_DV
ATTN_SCALE = HEAD_DK ** -0.5
ROPE_THETA = 10000.0
EPS = 1e-6

_C_POOL = 0
_C_BG = BRANCH_W
_C_CG = 2 * BRANCH_W
_C_HB = 3 * BRANCH_W
_C_VAL = 4 * BRANCH_W
_C_GL = 5 * BRANCH_W
_C_Q = 6 * BRANCH_W
_C_K = 7 * BRANCH_W
_C_V = 8 * BRANCH_W
_C_MGATE = 9 * BRANCH_W
_C_MRG = _C_MGATE + N_BRANCH * BRANCH_W
N_IN = _C_MRG + N_BRANCH * D_MODEL

_LANES = 128
_SUBLANES = 8
_VMEM_LIMIT = 56 * 1024 * 1024

_PROMPT_TILE = 256
_ATTN_TILE = 256
_ATTN_SUBTILES = 2
_SAMPLE_GROUP = 32

_BF16 = jnp.bfloat16
_F32 = jnp.float32
_LOG2E = math.log2(math.e)


def _rms(x, g):
    return x * lax.rsqrt(jnp.mean(x * x, axis=-1, keepdims=True) + EPS) * g


def _sigmoid(x):
    return 0.5 * jnp.tanh(0.5 * x) + 0.5


def _silu(x):
    return x * _sigmoid(x)


def _dot(a, b):
    return jnp.dot(a.astype(_BF16), b.astype(_BF16), preferred_element_type=_F32)


def _dot_nt(a, b):
    return lax.dot_general(a.astype(_BF16), b.astype(_BF16), (((1,), (1,)), ((), ())),
                           preferred_element_type=_F32)


def _lambda(lq_ref, lam_init):
    lq = lq_ref[...]
    a = jnp.sum(lq[0:1, :] * lq[1:2, :], axis=-1, keepdims=True)
    b = jnp.sum(lq[2:3, :] * lq[3:4, :], axis=-1, keepdims=True)
    return jnp.exp(a) - jnp.exp(b) + lam_init


def _mod_kernel(c_ref, w_ref, b_ref, o_ref):
    o_ref[...] = _dot(_silu(c_ref[...]), w_ref[...]) + b_ref[...]


def _modulation(c_all, w_ada, b_ada):
    depth = w_ada.shape[0]
    rows = c_all.shape[0]
    n_blk = 3
    return pl.pallas_call(
        _mod_kernel,
        grid=(depth, n_blk),
        in_specs=[
            pl.BlockSpec((rows, D_MODEL), lambda l, j: (0, 0)),
            pl.BlockSpec((None, D_MODEL, D_MODEL), lambda l, j: (l, 0, j)),
            pl.BlockSpec((None, 1, D_MODEL), lambda l, j: (l, 0, j)),
        ],
        out_specs=pl.BlockSpec((None, rows, D_MODEL), lambda l, j: (l, 0, j)),
        out_shape=jax.ShapeDtypeStruct((depth, rows, 3 * D_MODEL), _F32),
        compiler_params=pltpu.CompilerParams(dimension_semantics=("arbitrary", "arbitrary"),
                                             vmem_limit_bytes=_VMEM_LIMIT),
        name="adaln_mod",
    )(c_all, w_ada, b_ada.reshape(depth, 1, 3 * D_MODEL))


def _halo_steps(k_minus_1, stride):
    if stride == 1:
        return -(-k_minus_1 // _SUBLANES) * _SUBLANES
    return k_minus_1


_MIXER_INPUTS = 18


def _mixer_kernel(*refs, tm, stride, pos_base, n_tiles, n_alias, stacked):
    (x_ref, shift_ref, scale_ref, gpre_ref, win_ref, wpool_ref, pscale_ref, wsc_ref, wcc_ref, bcc_ref,
     gcn_ref, bcn_ref, wbr_ref, cos_ref, sin_ref, ppre_ref, spre_ref, cpre_ref) = refs[:_MIXER_INPUTS]
    outs = refs[_MIXER_INPUTS + n_alias:]
    q_ref, k_ref, v_ref, mabc_ref, gd_ref, sd_ref, pst_ref, sst_ref, cst_ref = outs[:9]
    n_out = 10 if stacked else 9
    extp, exts, extc, hbuf = outs[n_out:]
    i = pl.program_id(1)
    hp = _halo_steps(POOL_BUF, stride) * stride
    hs = _halo_steps(SCONV_K - 1, stride) * stride
    hc = _halo_steps(CCONV_K - 1, stride) * stride

    @pl.when(i == 0)
    def _():
        extp[0:hp, :] = ppre_ref[...]
        exts[0:hs, :] = spre_ref[...]
        extc[0:hc, :] = cpre_ref[...]

    x = x_ref[...]
    h = _rms(x, gpre_ref[...]) * (1.0 + scale_ref[...]) + shift_ref[...]
    hb = h.astype(_BF16)

    def proj(c0, n):
        return jnp.dot(hb, win_ref[:, c0:c0 + n], preferred_element_type=_F32)

    def gated_branch(y, n):
        ys = y * _silu(proj(_C_MGATE + n * BRANCH_W, BRANCH_W))
        br = _dot(ys, wbr_ref[n])
        return _sigmoid(proj(_C_MRG + n * D_MODEL, D_MODEL)) * br

    u_a = proj(_C_POOL, BRANCH_W)
    extp[hp:hp + tm, :] = u_a
    row = lax.broadcasted_iota(jnp.int32, (tm, 1), 0) + i * tm
    step = row if stride == 1 else lax.shift_right_logical(row, int(math.log2(stride)))
    pos = pos_base + step
    d_parts = []
    for g, win in enumerate(POOL_WINDOWS):
        sl = slice(g * POOL_GC, (g + 1) * POOL_GC)
        cur = extp[hp:hp + tm, sl]
        acc = cur
        for d in range(1, win):
            acc = acc + extp[hp - d * stride:hp - d * stride + tm, sl]
        cnt = jnp.minimum(win, pos + 1).astype(_F32)
        d_parts.append(acc / cnt - cur)
    d_pool = jnp.concatenate(d_parts, axis=-1)
    y_a = _dot(d_pool, wpool_ref[...]) * pscale_ref[...]
    merged = gated_branch(y_a, 0)

    z_b = proj(_C_CG, BRANCH_W) * proj(_C_HB, BRANCH_W)
    exts[hs:hs + tm, :] = z_b
    conv_b = wsc_ref[SCONV_K - 1:SCONV_K, :] * z_b
    for j in range(SCONV_K - 1):
        off = hs - (SCONV_K - 1 - j) * stride
        conv_b = conv_b + wsc_ref[j:j + 1, :] * exts[off:off + tm, :]
    y_b = proj(_C_BG, BRANCH_W) * conv_b
    merged = merged + gated_branch(y_b, 1)

    z_c = proj(_C_VAL, BRANCH_W) * _sigmoid(proj(_C_GL, BRANCH_W))
    extc[hc:hc + tm, :] = z_c
    groups = {}
    for j in range(CCONV_K):
        off = hc - (CCONV_K - 1 - j) * stride
        groups.setdefault(off % _SUBLANES, []).append((j, off - off % _SUBLANES))
    conv_parts = []
    for ci, c0 in enumerate(range(0, BRANCH_W, _LANES)):
        sl = slice(c0, c0 + _LANES)
        out = None
        for rem, taps in sorted(groups.items()):
            n = tm if rem == 0 else tm + _SUBLANES
            acc = None
            for j, base in taps:
                term = wcc_ref[j:j + 1, sl] * extc[base:base + n, sl]
                acc = term if acc is None else acc + term
            if rem:
                hbuf[ci, rem - 1] = acc
                acc = hbuf[ci, rem - 1, rem:rem + tm, :]
            out = acc if out is None else out + acc
        conv_parts.append(out)
    conv_c = jnp.concatenate(conv_parts, axis=-1) + bcc_ref[...]
    mu = jnp.mean(conv_c, axis=-1, keepdims=True)
    xc = conv_c - mu
    ln = xc * lax.rsqrt(jnp.mean(xc * xc, axis=-1, keepdims=True) + EPS) * gcn_ref[...] + bcn_ref[...]
    y_c = _silu(ln)
    merged = merged + gated_branch(y_c, 2)
    mabc_ref[...] = merged.astype(mabc_ref.dtype)

    cos = jnp.concatenate([cos_ref[...]] * (BRANCH_W // _LANES), axis=-1)
    sin = jnp.concatenate([sin_ref[...]] * (BRANCH_W // _LANES), axis=-1)
    lane = lax.broadcasted_iota(jnp.int32, (tm, BRANCH_W), 1)
    first_half = (lane % HEAD_DK) < (HEAD_DK // 2)

    def rope(t):
        fwd = pltpu.roll(t, BRANCH_W - HEAD_DK // 2, axis=1)
        bwd = pltpu.roll(t, HEAD_DK // 2, axis=1)
        return t * cos + jnp.where(first_half, fwd, bwd) * sin

    q_ref[...] = (rope(proj(_C_Q, BRANCH_W)) * (ATTN_SCALE * _LOG2E)).astype(q_ref.dtype)
    k_rot = rope(proj(_C_K, BRANCH_W))
    k_ref[...] = k_rot.astype(k_ref.dtype)
    v_new = proj(_C_V, BRANCH_W)
    if stacked:
        outs[9][...] = k_rot.T
        for hd in range(N_HEADS):
            v_ref[pl.ds(hd, tm, stride=N_HEADS), :] = v_new[:, hd * HEAD_DV:(hd + 1) * HEAD_DV]
    else:
        v_ref[...] = v_new
    gd_ref[...] = _silu(proj(_C_MGATE + 3 * BRANCH_W, BRANCH_W)).astype(gd_ref.dtype)
    sd_ref[...] = _sigmoid(proj(_C_MRG + 3 * D_MODEL, D_MODEL)).astype(sd_ref.dtype)

    new_p = extp[tm:tm + hp, :]
    new_s = exts[tm:tm + hs, :]
    new_c = extc[tm:tm + hc, :]
    if n_tiles > 1:
        extp[0:hp, :] = new_p
        exts[0:hs, :] = new_s
        extc[0:hc, :] = new_c

    @pl.when(i == n_tiles - 1)
    def _():
        pst_ref[...] = new_p
        sst_ref[...] = new_s
        cst_ref[...] = new_c


def _mixer(x, shift, scale, g_pre, w_in, w_pool_bd, pool_scale, w_sconv, w_cconv, b_cconv, g_cnorm,
           b_cnorm, w_branch, cos, sin, pool_pre, sconv_pre, cconv_pre, *, layer, tm, stride, pos_base,
           kv_stack=None):
    depth = w_in.shape[0]
    nb, rows, _ = x.shape
    n_tiles = rows // tm
    hp = _halo_steps(POOL_BUF, stride) * stride
    hs = _halo_steps(SCONV_K - 1, stride) * stride
    hc = _halo_steps(CCONV_K - 1, stride) * stride
    mod_rows = shift.shape[1]
    once = pl.Buffered(1)

    def const(shape):
        nd = len(shape)
        return pl.BlockSpec(shape, lambda b, i: (0,) * nd, pipeline_mode=once)

    def per_layer(shape):
        nd = len(shape)
        return pl.BlockSpec((None,) + shape, lambda b, i: (layer,) + (0,) * nd, pipeline_mode=once)

    def per_b(shape):
        return pl.BlockSpec((None,) + shape, lambda b, i: (b, 0, 0))

    def tiled(width):
        return pl.BlockSpec((None, tm, width), lambda b, i: (b, i, 0))

    mod_spec = per_b((1, D_MODEL)) if mod_rows == 1 else tiled(D_MODEL)
    row_vec = lambda a: a.reshape(1, -1)
    stacked = kv_stack is not None
    n_alias = len(kv_stack) if stacked else 0
    kern = functools.partial(_mixer_kernel, tm=tm, stride=stride, pos_base=pos_base, n_tiles=n_tiles,
                             n_alias=n_alias, stacked=stacked)
    out_shapes = [jax.ShapeDtypeStruct((nb, rows, w), dt)
                  for w, dt in ((BRANCH_W, _BF16), (BRANCH_W, _BF16 if stacked else _F32), (BRANCH_W, _F32),
                                (D_MODEL, _BF16), (BRANCH_W, _BF16), (D_MODEL, _BF16))]
    out_shapes += [jax.ShapeDtypeStruct((nb, hh, BRANCH_W), _F32) for hh in (hp, hs, hc)]
    out_specs = [tiled(BRANCH_W), tiled(BRANCH_W), tiled(BRANCH_W), tiled(D_MODEL), tiled(BRANCH_W),
                 tiled(D_MODEL), per_b((hp, BRANCH_W)), per_b((hs, BRANCH_W)), per_b((hc, BRANCH_W))]
    aliases = {}
    if stacked:
        out_shapes[2] = jax.ShapeDtypeStruct((depth, nb, rows * N_HEADS, HEAD_DV), _F32)
        out_specs[2] = pl.BlockSpec((None, None, tm * N_HEADS, HEAD_DV), lambda b, i: (layer, b, i, 0))
        out_shapes.append(jax.ShapeDtypeStruct((depth, nb, BRANCH_W, rows), _F32))
        out_specs.append(pl.BlockSpec((None, None, BRANCH_W, tm), lambda b, i: (layer, b, 0, i)))
        if n_alias:
            aliases = {_MIXER_INPUTS: 2, _MIXER_INPUTS + 1: 9}
    return pl.pallas_call(
        kern,
        grid=(nb, n_tiles),
        in_specs=[
            tiled(D_MODEL), mod_spec, mod_spec, const((1, D_MODEL)),
            per_layer((D_MODEL, N_IN)), const((BRANCH_W, BRANCH_W)), const((1, BRANCH_W)),
            const((SCONV_K, BRANCH_W)), const((CCONV_K, BRANCH_W)), const((1, BRANCH_W)),
            const((1, BRANCH_W)), const((1, BRANCH_W)), per_layer((N_BRANCH, BRANCH_W, D_MODEL)),
            pl.BlockSpec((tm, _LANES), lambda b, i: (i, 0)), pl.BlockSpec((tm, _LANES), lambda b, i: (i, 0)),
            per_b((hp, BRANCH_W)), per_b((hs, BRANCH_W)), per_b((hc, BRANCH_W)),
        ] + [pl.BlockSpec(memory_space=pl.ANY)] * n_alias,
        out_specs=out_specs,
        out_shape=out_shapes,
        input_output_aliases=aliases,
        scratch_shapes=[pltpu.VMEM((hp + tm, BRANCH_W), _F32), pltpu.VMEM((hs + tm, BRANCH_W), _F32),
                        pltpu.VMEM((hc + tm, BRANCH_W), _F32),
                        pltpu.VMEM((BRANCH_W // _LANES, _SUBLANES - 1, tm + _SUBLANES, _LANES), _F32)],
        compiler_params=pltpu.CompilerParams(dimension_semantics=("arbitrary", "arbitrary"),
                                             vmem_limit_bytes=_VMEM_LIMIT),
        name="mixer_s%d" % stride,
    )(x, shift, scale, row_vec(g_pre), w_in, w_pool_bd, row_vec(pool_scale), w_sconv, w_cconv,
      row_vec(b_cconv), row_vec(g_cnorm), row_vec(b_cnorm), w_branch, cos, sin,
      pool_pre, sconv_pre, cconv_pre, *(kv_stack or ()))


def _attn_prompt_kernel(lq_ref, q_ref, k_ref, v_ref, o_ref, vt_ref, qs_ref, acc_ref, m_ref, l_ref, st_ref, *,
                        tq, n_sub, lam_init):
    i = pl.program_id(1)
    n_kv = k_ref.shape[0] // tq
    lam = _lambda(lq_ref, lam_init)

    @pl.when(i == 0)
    def _():
        for h in range(N_HEADS):
            for c in range(n_kv):
                vt_ref[h, c] = v_ref[pl.ds(c * tq * N_HEADS + h, tq, stride=N_HEADS), :].T.astype(_BF16)

    lane = lax.broadcasted_iota(jnp.int32, (tq, HEAD_DV), 1)
    first = lane < HEAD_DK
    for s in range(n_sub):
        for h in range(N_HEADS):
            qh = q_ref[s * tq:(s + 1) * tq, h * HEAD_DV:(h + 1) * HEAD_DV].astype(_F32)
            qs_ref[s * N_HEADS + h, 0:tq, :] = jnp.where(first, qh, 0.0).astype(_BF16)
            qs_ref[s * N_HEADS + h, tq:2 * tq, :] = jnp.where(first, 0.0, qh).astype(_BF16)
    acc_ref[...] = jnp.zeros_like(acc_ref)
    m_ref[...] = jnp.full(m_ref.shape, -jnp.inf, _F32)
    l_ref[...] = jnp.zeros_like(l_ref)
    causal = (lax.broadcasted_iota(jnp.int32, (tq, 2 * tq), 0)
              <= lax.broadcasted_iota(jnp.int32, (tq, 2 * tq), 1) % tq)

    def tile(j, subs):
        r0 = pl.multiple_of(j * tq, tq)
        for h in range(N_HEADS):
            kj = k_ref[pl.ds(r0, tq), h * HEAD_DV:(h + 1) * HEAD_DV].astype(_BF16)
            for s, _ in subs:
                st_ref[s * N_HEADS + h] = _dot_nt(kj, qs_ref[s * N_HEADS + h])
        for h in range(N_HEADS):
            vtj = vt_ref[h, j]
            for s, masked in subs:
                c = s * N_HEADS + h
                st = st_ref[c]
                if masked:
                    st = jnp.where(causal, st, -jnp.inf)
                m = m_ref[c:c + 1, :]
                m_new = jnp.maximum(m, jnp.max(st, axis=0, keepdims=True))
                p = jnp.exp2(st - m_new)
                alpha = jnp.exp2(m - m_new)
                m_ref[c:c + 1, :] = m_new
                l_ref[c:c + 1, :] = alpha * l_ref[c:c + 1, :] + jnp.sum(p, axis=0, keepdims=True)
                acc_ref[c] = alpha * acc_ref[c] + jnp.dot(vtj, p.astype(_BF16), preferred_element_type=_F32)

    def body(j, carry):
        tile(j, [(s, False) for s in range(n_sub)])
        return carry

    lax.fori_loop(0, n_sub * i, body, 0)
    for t in range(n_sub):
        tile(n_sub * i + t, [(t, True)] + [(s, False) for s in range(t + 1, n_sub)])
    for s in range(n_sub):
        for h in range(N_HEADS):
            c = s * N_HEADS + h
            a = acc_ref[c] * (1.0 / l_ref[c:c + 1, :])
            o_t = a[:, 0:tq] - lam * a[:, tq:2 * tq]
            o_ref[s * tq:(s + 1) * tq, h * HEAD_DV:(h + 1) * HEAD_DV] = o_t.T.astype(o_ref.dtype)


def _attn_prompt(lambda_qk_l, q, k, v_all, layer, lam_init):
    nb, s, _ = q.shape
    tq = _ATTN_TILE
    n_sub = _ATTN_SUBTILES
    n_chain = n_sub * N_HEADS
    kern = functools.partial(_attn_prompt_kernel, tq=tq, n_sub=n_sub, lam_init=lam_init)
    return pl.pallas_call(
        kern,
        grid=(nb, s // (n_sub * tq)),
        in_specs=[
            pl.BlockSpec((4, HEAD_DK), lambda b, i: (0, 0)),
            pl.BlockSpec((None, n_sub * tq, BRANCH_W), lambda b, i: (b, i, 0)),
            pl.BlockSpec((None, s, BRANCH_W), lambda b, i: (b, 0, 0)),
            pl.BlockSpec((None, None, s * N_HEADS, HEAD_DV), lambda b, i: (layer, b, 0, 0)),
        ],
        out_specs=pl.BlockSpec((None, n_sub * tq, BRANCH_W), lambda b, i: (b, i, 0)),
        out_shape=jax.ShapeDtypeStruct((nb, s, BRANCH_W), _BF16),
        scratch_shapes=[pltpu.VMEM((N_HEADS, s // tq, HEAD_DV, tq), _BF16),
                        pltpu.VMEM((n_chain, 2 * tq, HEAD_DV), _BF16),
                        pltpu.VMEM((n_chain, HEAD_DV, 2 * tq), _F32),
                        pltpu.VMEM((n_chain, 2 * tq), _F32), pltpu.VMEM((n_chain, 2 * tq), _F32),
                        pltpu.VMEM((n_chain, tq, 2 * tq), _F32)],
        compiler_params=pltpu.CompilerParams(dimension_semantics=("arbitrary", "arbitrary"),
                                             vmem_limit_bytes=_VMEM_LIMIT),
        name="attn_prompt",
    )(lambda_qk_l, q, k, v_all)


def _attn_sample_kernel(pt_ref, lq_ref, q_ref, kn_ref, vn_ref, *rest, n_pages, page, t_new, lam_init):
    del pt_ref
    k_pages = rest[:n_pages]
    v_pages = rest[n_pages:2 * n_pages]
    o_ref = rest[2 * n_pages]
    s_ref, knew, vnew = rest[2 * n_pages + 1:]
    lam = _lambda(lq_ref, lam_init)
    n_rows = 2 * N_HEADS * t_new
    half = N_HEADS * t_new

    r = lax.broadcasted_iota(jnp.int32, (n_rows, BRANCH_W), 0)
    c = lax.broadcasted_iota(jnp.int32, (n_rows, BRANCH_W), 1)
    q_rep = jnp.zeros((n_rows, BRANCH_W), _F32)
    for t in range(t_new):
        q_rep = jnp.where(r % t_new == t, q_ref[t:t + 1, :], q_rep)
    chunk_of_row = 2 * ((r % half) // t_new) + r // half
    wq = jnp.where(c // HEAD_DK == chunk_of_row, q_rep, 0.0).astype(_BF16)

    for p in range(n_pages):
        s_ref[:, p * page:(p + 1) * page] = _dot(wq, k_pages[p][...])

    @pl.when(pl.program_id(0) == 0)
    def _():
        knew[...] = jnp.zeros_like(knew)
        vnew[...] = jnp.zeros_like(vnew)

    knew[0:t_new, :] = kn_ref[...]
    vnew[0:t_new, :] = vn_ref[...]
    s_new = _dot_nt(wq, knew[...])
    key = lax.broadcasted_iota(jnp.int32, (n_rows, page), 1)
    tok = lax.broadcasted_iota(jnp.int32, (n_rows, page), 0) % t_new
    s_ref[:, n_pages * page:(n_pages + 1) * page] = jnp.where(key <= tok, s_new, -jnp.inf)

    s = s_ref[...]
    e = jnp.exp2(s - jnp.max(s, axis=-1, keepdims=True))
    pr = e / jnp.sum(e, axis=-1, keepdims=True)
    wts = (pr[0:half, :] - lam * pr[half:n_rows, :]).astype(_BF16)

    acc = jnp.dot(wts[:, n_pages * page:(n_pages + 1) * page], vnew[...].astype(_BF16),
                  preferred_element_type=_F32)
    for p in range(n_pages):
        v_cat = jnp.concatenate([v_pages[p][pl.ds(h, page, stride=N_HEADS), :] for h in range(N_HEADS)],
                                axis=-1).astype(_BF16)
        acc = acc + jnp.dot(wts[:, p * page:(p + 1) * page], v_cat, preferred_element_type=_F32)
    ar = lax.broadcasted_iota(jnp.int32, (half, BRANCH_W), 0)
    ac = lax.broadcasted_iota(jnp.int32, (half, BRANCH_W), 1)
    own = jnp.where(ar // t_new == ac // HEAD_DV, acc, 0.0)
    tot = own
    for h in range(1, N_HEADS):
        tot = tot + pltpu.roll(own, h * t_new, axis=0)
    o_ref[...] = tot[0:t_new, :]


def _attn_sample(layer, page_table, cache_kt, cache_v, lambda_qk_l, q, k_new, v_new, lam_init):
    nb, t_new, _ = q.shape
    n_pages = page_table.shape[1]
    page = cache_kt.shape[3]

    def k_spec(j):
        return pl.BlockSpec((None, None, BRANCH_W, page), lambda b, pt: (layer, pt[b, j], 0, 0))

    def v_spec(j):
        return pl.BlockSpec((None, None, N_HEADS * page, HEAD_DV), lambda b, pt: (layer, pt[b, j], 0, 0))

    tok_spec = pl.BlockSpec((None, t_new, BRANCH_W), lambda b, pt: (b, 0, 0))
    kern = functools.partial(_attn_sample_kernel, n_pages=n_pages, page=page, t_new=t_new, lam_init=lam_init)
    n_rows = 2 * N_HEADS * t_new
    grid_spec = pltpu.PrefetchScalarGridSpec(
        num_scalar_prefetch=1,
        grid=(nb,),
        in_specs=[pl.BlockSpec((4, HEAD_DK), lambda b, pt: (0, 0)), tok_spec, tok_spec, tok_spec]
        + [k_spec(j) for j in range(n_pages)] + [v_spec(j) for j in range(n_pages)],
        out_specs=tok_spec,
        scratch_shapes=[pltpu.VMEM((n_rows, (n_pages + 1) * page), _F32),
                        pltpu.VMEM((page, BRANCH_W), _F32), pltpu.VMEM((page, BRANCH_W), _F32)],
    )
    return pl.pallas_call(
        kern,
        grid_spec=grid_spec,
        out_shape=jax.ShapeDtypeStruct((nb, t_new, BRANCH_W), _F32),
        compiler_params=pltpu.CompilerParams(dimension_semantics=("arbitrary",),
                                             vmem_limit_bytes=_VMEM_LIMIT),
        name="attn_sample",
    )(page_table, lambda_qk_l, q, k_new, v_new, *([cache_kt] * n_pages), *([cache_v] * n_pages))


def _output_kernel(o_ref, gd_ref, sd_ref, mabc_ref, x_ref, gate_ref, gsub_ref, wbr_ref, wo_ref, gpost_ref,
                   y_ref, *, lam_init):
    o = o_ref[...].astype(_F32)
    parts = [_rms(o[:, h * HEAD_DV:(h + 1) * HEAD_DV], gsub_ref[...]) * (1.0 - lam_init)
             for h in range(N_HEADS)]
    ys = jnp.concatenate(parts, axis=-1) * gd_ref[...].astype(_F32)
    merged = mabc_ref[...].astype(_F32) + sd_ref[...].astype(_F32) * _dot(ys, wbr_ref[...])
    out = _rms(_dot(merged, wo_ref[...]), gpost_ref[...])
    y_ref[...] = x_ref[...] + gate_ref[...] * out


def _output(o, gd, sd, mabc, x, gate, g_subln, w_branch, w_o, g_post, *, layer, tm, lam_init):
    nb, rows, _ = x.shape
    once = pl.Buffered(1)

    def tiled(width):
        return pl.BlockSpec((None, tm, width), lambda b, i: (b, i, 0))

    gate_spec = (pl.BlockSpec((None, 1, D_MODEL), lambda b, i: (b, 0, 0)) if gate.shape[1] == 1
                 else tiled(D_MODEL))
    kern = functools.partial(_output_kernel, lam_init=lam_init)
    return pl.pallas_call(
        kern,
        grid=(nb, rows // tm),
        in_specs=[
            tiled(BRANCH_W), tiled(BRANCH_W), tiled(D_MODEL), tiled(D_MODEL), tiled(D_MODEL), gate_spec,
            pl.BlockSpec((1, HEAD_DV), lambda b, i: (0, 0)),
            pl.BlockSpec((None, None, BRANCH_W, D_MODEL), lambda b, i: (layer, N_BRANCH - 1, 0, 0),
                         pipeline_mode=once),
            pl.BlockSpec((None, D_MODEL, D_MODEL), lambda b, i: (layer, 0, 0), pipeline_mode=once),
            pl.BlockSpec((1, D_MODEL), lambda b, i: (0, 0)),
        ],
        out_specs=tiled(D_MODEL),
        out_shape=jax.ShapeDtypeStruct((nb, rows, D_MODEL), _F32),
        compiler_params=pltpu.CompilerParams(dimension_semantics=("arbitrary", "arbitrary"),
                                             vmem_limit_bytes=_VMEM_LIMIT),
        name="output_proj",
    )(o, gd, sd, mabc, x, gate, g_subln.reshape(1, -1), w_branch, w_o, g_post.reshape(1, -1))


def _to_time_major(a):
    nb, t, c = a.shape
    g = nb // _SAMPLE_GROUP
    return a.reshape(g, _SAMPLE_GROUP, t, c).transpose(0, 2, 1, 3).reshape(g, t * _SAMPLE_GROUP, c)


def _from_time_major(a, t):
    g, rows, c = a.shape
    return a.reshape(g, t, _SAMPLE_GROUP, c).transpose(0, 2, 1, 3).reshape(g * _SAMPLE_GROUP, t, c)


def _rope_tables(pos):
    half = HEAD_DK // 2
    inv = jnp.power(ROPE_THETA, -jnp.arange(half, dtype=_F32) / half)
    ang = pos.astype(_F32)[:, None] * inv[None, :]
    cos, sin = jnp.cos(ang), jnp.sin(ang)
    reps = _LANES // HEAD_DK
    cos_t = jnp.tile(jnp.concatenate([cos, cos], axis=-1), (1, reps))
    sin_t = jnp.tile(jnp.concatenate([-sin, sin], axis=-1), (1, reps))
    return cos_t, sin_t


def _block_diag(w):
    g, a, b = w.shape
    out = jnp.zeros((g * a, g * b), w.dtype)
    for n in range(g):
        out = out.at[n * a:(n + 1) * a, n * b:(n + 1) * b].set(w[n])
    return out


def kernel(x_prompt, x_sample, cache_k, cache_v, page_table, state_pool, state_sconv, state_cconv,
           c_prompt, c_sample, w_ada, b_ada, g_pre, g_post, w_in, w_pool, pool_scale, w_sconv, w_cconv,
           b_cconv, g_cnorm, b_cnorm, lambda_qk, g_subln, w_branch, w_o):
    bp, sp, _ = x_prompt.shape
    db, ds, _ = x_sample.shape
    depth = w_in.shape[0]
    n_pool, page = cache_k.shape[1], cache_k.shape[2]
    past = page_table.shape[1] * page
    grp = _SAMPLE_GROUP

    mod = _modulation(jnp.concatenate([c_prompt, c_sample], axis=0), w_ada, b_ada)
    cos_p, sin_p = _rope_tables(jnp.arange(sp))
    cos_s, sin_s = _rope_tables(jnp.repeat(past + jnp.arange(ds), grp))
    cache_kt = cache_k.transpose(0, 1, 3, 4, 5, 2).reshape(depth, n_pool, BRANCH_W, page)
    cache_v2 = cache_v.reshape(depth, n_pool, page * N_HEADS, HEAD_DV)

    hp1, hs1, hc1 = (_halo_steps(n, 1) for n in (POOL_BUF, SCONV_K - 1, CCONV_K - 1))
    zero_pre = [jnp.zeros((bp, hh, BRANCH_W), _F32) for hh in (hp1, hs1, hc1)]

    w_in_b, w_br_b, w_o_b = (w.astype(_BF16) for w in (w_in, w_branch, w_o))
    xp = x_prompt
    xs = _to_time_major(x_sample)
    outs_p = [[] for _ in range(3)]
    outs_s = [[] for _ in range(5)]
    kv_stack = ()
    for l in range(depth):
        lam_init = 0.8 - 0.6 * math.exp(-0.3 * l)
        w_pool_l = _block_diag(w_pool[l]).astype(_BF16)
        weights = (g_pre[l], w_in_b, w_pool_l, pool_scale[l], w_sconv[l], w_cconv[l], b_cconv[l],
                   g_cnorm[l], b_cnorm[l], w_br_b)

        mod_p = mod[l, :bp]
        shift, scale, gate = (mod_p[:, None, n * D_MODEL:(n + 1) * D_MODEL] for n in range(3))
        q, k, v_all, mabc, gd, sd, pst, sst, cst, kt_all = _mixer(
            xp, shift, scale, *weights, cos_p, sin_p, *zero_pre, layer=l, tm=_PROMPT_TILE, stride=1, pos_base=0,
            kv_stack=kv_stack)
        kv_stack = (v_all, kt_all)
        o = _attn_prompt(lambda_qk[l], q, k, v_all, l, lam_init)
        xp = _output(o, gd, sd, mabc, xp, gate, g_subln[l], w_br_b, w_o_b, g_post[l],
                     layer=l, tm=_PROMPT_TILE, lam_init=lam_init)
        for lst, t in zip(outs_p, (pst[:, hp1 - POOL_BUF:], sst[:, hs1 - (SCONV_K - 1):],
                                   cst[:, hc1 - (CCONV_K - 1):])):
            lst.append(t)

        mod_s = mod[l, bp:]
        shift, scale, gate = (_to_time_major(jnp.broadcast_to(mod_s[:, None, n * D_MODEL:(n + 1) * D_MODEL],
                                                              (db, ds, D_MODEL))) for n in range(3))
        pre = [_to_time_major(st[l]) for st in (state_pool, state_sconv, state_cconv)]
        q, k, v, mabc, gd, sd, pst, sst, cst = _mixer(
            xs, shift, scale, *weights, cos_s, sin_s, *pre, layer=l, tm=ds * grp, stride=grp, pos_base=past)
        q_b, k_b, v_b = (_from_time_major(t, ds) for t in (q.astype(_F32), k, v))
        o_b = _attn_sample(l, page_table, cache_kt, cache_v2, lambda_qk[l], q_b, k_b, v_b, lam_init)
        xs = _output(_to_time_major(o_b.astype(_BF16)), gd, sd, mabc, xs, gate, g_subln[l], w_br_b, w_o_b, g_post[l],
                     layer=l, tm=ds * grp, lam_init=lam_init)
        for lst, t in zip(outs_s, (k_b.reshape(db, ds, N_HEADS, 2, HEAD_DK), v_b.reshape(db, ds, N_HEADS, HEAD_DV),
                                   _from_time_major(pst, POOL_BUF), _from_time_major(sst, SCONV_K - 1),
                                   _from_time_major(cst, CCONV_K - 1))):
            lst.append(t)

    y_sample = _from_time_major(xs, ds)
    v_all, kt_all = kv_stack
    k_prompt = kt_all.reshape(depth, bp, N_HEADS, 2, HEAD_DK, sp).transpose(0, 1, 5, 2, 3, 4)
    v_prompt = v_all.reshape(depth, bp, sp, N_HEADS, HEAD_DV)
    stacked_p = [jnp.stack(t) for t in outs_p]
    stacked_s = [jnp.stack(t) for t in outs_s]
    return (xp, y_sample, k_prompt, v_prompt, *stacked_p, *stacked_s)
```

```python
import functools
import math

import jax
import jax.numpy as jnp
from jax import lax
from jax.experimental import pallas as pl
from jax.experimental.pallas import tpu as pltpu

D_MODEL = 1024
N_BRANCH = 4
BRANCH_W = D_MODEL // 2
POOL_WINDOWS = (2, 4, 8, 16)
POOL_GC = BRANCH_W // len(POOL_WINDOWS)
POOL_BUF = max(POOL_WINDOWS) - 1
SCONV_K = 3
CCONV_K = 31
HEAD_DK = 64
HEAD_DV = 2 * HEAD_DK
N_HEADS = BRANCH_W // HEAD_DV
ATTN_SCALE = HEAD_DK ** -0.5
ROPE_THETA = 10000.0
EPS = 1e-6

_C_POOL = 0
_C_BG = BRANCH_W
_C_CG = 2 * BRANCH_W
_C_HB = 3 * BRANCH_W
_C_VAL = 4 * BRANCH_W
_C_GL = 5 * BRANCH_W
_C_Q = 6 * BRANCH_W
_C_K = 7 * BRANCH_W
_C_V = 8 * BRANCH_W
_C_MGATE = 9 * BRANCH_W
_C_MRG = _C_MGATE + N_BRANCH * BRANCH_W
N_IN = _C_MRG + N_BRANCH * D_MODEL

_LANES = 128
_SUBLANES = 8
_VMEM_LIMIT = 56 * 1024 * 1024

_PROMPT_TILE = 256
_OUTPUT_TILE = 512
_ATTN_TILE = 256
_ATTN_SUBTILES = 4
_SAMPLE_GROUP = 32

_BF16 = jnp.bfloat16
_F32 = jnp.float32
_LOG2E = math.log2(math.e)


def _rms(x, g):
    return x * lax.rsqrt(jnp.mean(x * x, axis=-1, keepdims=True) + EPS) * g


def _sigmoid(x):
    return 0.5 * jnp.tanh(0.5 * x) + 0.5


def _silu(x):
    return x * _sigmoid(x)


def _dot(a, b):
    return jnp.dot(a.astype(_BF16), b.astype(_BF16), preferred_element_type=_F32)


def _dot_nt(a, b):
    return lax.dot_general(a.astype(_BF16), b.astype(_BF16), (((1,), (1,)), ((), ())),
                           preferred_element_type=_F32)


def _lambda(lq_ref, lam_init):
    lq = lq_ref[...]
    a = jnp.sum(lq[0:1, :] * lq[1:2, :], axis=-1, keepdims=True)
    b = jnp.sum(lq[2:3, :] * lq[3:4, :], axis=-1, keepdims=True)
    return jnp.exp(a) - jnp.exp(b) + lam_init


def _mod_kernel(c_ref, w_ref, b_ref, o_ref):
    o_ref[...] = _dot(_silu(c_ref[...]), w_ref[...]) + b_ref[...]


def _modulation(c_all, w_ada, b_ada):
    depth = w_ada.shape[0]
    rows = c_all.shape[0]
    n_blk = 3
    return pl.pallas_call(
        _mod_kernel,
        grid=(depth, n_blk),
        in_specs=[
            pl.BlockSpec((rows, D_MODEL), lambda l, j: (0, 0)),
            pl.BlockSpec((None, D_MODEL, D_MODEL), lambda l, j: (l, 0, j)),
            pl.BlockSpec((None, 1, D_MODEL), lambda l, j: (l, 0, j)),
        ],
        out_specs=pl.BlockSpec((None, rows, D_MODEL), lambda l, j: (l, 0, j)),
        out_shape=jax.ShapeDtypeStruct((depth, rows, 3 * D_MODEL), _F32),
        compiler_params=pltpu.CompilerParams(dimension_semantics=("arbitrary", "arbitrary"),
                                             vmem_limit_bytes=_VMEM_LIMIT),
        name="adaln_mod",
    )(c_all, w_ada, b_ada.reshape(depth, 1, 3 * D_MODEL))


def _halo_steps(k_minus_1, stride):
    if stride == 1:
        return -(-k_minus_1 // _SUBLANES) * _SUBLANES
    return k_minus_1


_MIXER_INPUTS = 18


def _mixer_kernel(*refs, tm, stride, pos_base, n_tiles, n_alias, stacked):
    (x_ref, shift_ref, scale_ref, gpre_ref, win_ref, wpool_ref, pscale_ref, wsc_ref, wcc_ref, bcc_ref,
     gcn_ref, bcn_ref, wbr_ref, cos_ref, sin_ref, ppre_ref, spre_ref, cpre_ref) = refs[:_MIXER_INPUTS]
    outs = refs[_MIXER_INPUTS + n_alias:]
    q_ref, k_ref, v_ref, mabc_ref, gd_ref, sd_ref, pst_ref, sst_ref, cst_ref = outs[:9]
    n_out = 10 if stacked else 9
    extp, exts, extc, hbuf = outs[n_out:]
    i = pl.program_id(1)
    hp = _halo_steps(POOL_BUF, stride) * stride
    hs = _halo_steps(SCONV_K - 1, stride) * stride
    hc = _halo_steps(CCONV_K - 1, stride) * stride

    @pl.when(i == 0)
    def _():
        extp[0:hp, :] = ppre_ref[...]
        exts[0:hs, :] = spre_ref[...]
        extc[0:hc, :] = cpre_ref[...]

    x = x_ref[...]
    h = _rms(x, gpre_ref[...]) * (1.0 + scale_ref[...]) + shift_ref[...]
    hb = h.astype(_BF16)

    def proj(c0, n):
        return jnp.dot(hb, win_ref[:, c0:c0 + n], preferred_element_type=_F32)

    def gated_branch(y, n):
        ys = y * _silu(proj(_C_MGATE + n * BRANCH_W, BRANCH_W))
        br = _dot(ys, wbr_ref[n])
        return _sigmoid(proj(_C_MRG + n * D_MODEL, D_MODEL)) * br

    u_a = proj(_C_POOL, BRANCH_W)
    extp[hp:hp + tm, :] = u_a
    row = lax.broadcasted_iota(jnp.int32, (tm, 1), 0) + i * tm
    step = row if stride == 1 else lax.shift_right_logical(row, int(math.log2(stride)))
    pos = pos_base + step
    d_parts = []
    for g, win in enumerate(POOL_WINDOWS):
        sl = slice(g * POOL_GC, (g + 1) * POOL_GC)
        cur = extp[hp:hp + tm, sl]
        acc = cur
        for d in range(1, win):
            acc = acc + extp[hp - d * stride:hp - d * stride + tm, sl]
        cnt = jnp.minimum(win, pos + 1).astype(_F32)
        d_parts.append(acc / cnt - cur)
    d_pool = jnp.concatenate(d_parts, axis=-1)
    y_a = _dot(d_pool, wpool_ref[...]) * pscale_ref[...]
    merged = gated_branch(y_a, 0)

    z_b = proj(_C_CG, BRANCH_W) * proj(_C_HB, BRANCH_W)
    exts[hs:hs + tm, :] = z_b
    conv_b = wsc_ref[SCONV_K - 1:SCONV_K, :] * z_b
    for j in range(SCONV_K - 1):
        off = hs - (SCONV_K - 1 - j) * stride
        conv_b = conv_b + wsc_ref[j:j + 1, :] * exts[off:off + tm, :]
    y_b = proj(_C_BG, BRANCH_W) * conv_b
    merged = merged + gated_branch(y_b, 1)

    z_c = proj(_C_VAL, BRANCH_W) * _sigmoid(proj(_C_GL, BRANCH_W))
    extc[hc:hc + tm, :] = z_c
    groups = {}
    for j in range(CCONV_K):
        off = hc - (CCONV_K - 1 - j) * stride
        groups.setdefault(off % _SUBLANES, []).append((j, off - off % _SUBLANES))
    conv_parts = []
    for ci, c0 in enumerate(range(0, BRANCH_W, _LANES)):
        sl = slice(c0, c0 + _LANES)
        out = None
        for rem, taps in sorted(groups.items()):
            n = tm if rem == 0 else tm + _SUBLANES
            acc = None
            for j, base in taps:
                term = wcc_ref[j:j + 1, sl] * extc[base:base + n, sl]
                acc = term if acc is None else acc + term
            if rem:
                hbuf[ci, rem - 1] = acc
                acc = hbuf[ci, rem - 1, rem:rem + tm, :]
            out = acc if out is None else out + acc
        conv_parts.append(out)
    conv_c = jnp.concatenate(conv_parts, axis=-1) + bcc_ref[...]
    mu = jnp.mean(conv_c, axis=-1, keepdims=True)
    xc = conv_c - mu
    ln = xc * lax.rsqrt(jnp.mean(xc * xc, axis=-1, keepdims=True) + EPS) * gcn_ref[...] + bcn_ref[...]
    y_c = _silu(ln)
    merged = merged + gated_branch(y_c, 2)
    mabc_ref[...] = merged.astype(mabc_ref.dtype)

    cos = jnp.concatenate([cos_ref[...]] * (BRANCH_W // _LANES), axis=-1)
    sin = jnp.concatenate([sin_ref[...]] * (BRANCH_W // _LANES), axis=-1)
    lane = lax.broadcasted_iota(jnp.int32, (tm, BRANCH_W), 1)
    first_half = (lane % HEAD_DK) < (HEAD_DK // 2)

    def rope(t):
        fwd = pltpu.roll(t, BRANCH_W - HEAD_DK // 2, axis=1)
        bwd = pltpu.roll(t, HEAD_DK // 2, axis=1)
        return t * cos + jnp.where(first_half, fwd, bwd) * sin

    q_ref[...] = (rope(proj(_C_Q, BRANCH_W)) * (ATTN_SCALE * _LOG2E)).astype(q_ref.dtype)
    k_rot = rope(proj(_C_K, BRANCH_W))
    k_ref[...] = k_rot.astype(k_ref.dtype)
    v_new = proj(_C_V, BRANCH_W)
    if stacked:
        outs[9][...] = k_rot.T
        for hd in range(N_HEADS):
            v_ref[pl.ds(hd, tm, stride=N_HEADS), :] = v_new[:, hd * HEAD_DV:(hd + 1) * HEAD_DV]
    else:
        v_ref[...] = v_new
    gd_ref[...] = _silu(proj(_C_MGATE + 3 * BRANCH_W, BRANCH_W)).astype(gd_ref.dtype)
    sd_ref[...] = _sigmoid(proj(_C_MRG + 3 * D_MODEL, D_MODEL)).astype(sd_ref.dtype)

    new_p = extp[tm:tm + hp, :]
    new_s = exts[tm:tm + hs, :]
    new_c = extc[tm:tm + hc, :]
    if n_tiles > 1:
        extp[0:hp, :] = new_p
        exts[0:hs, :] = new_s
        extc[0:hc, :] = new_c

    @pl.when(i == n_tiles - 1)
    def _():
        pst_ref[...] = new_p
        sst_ref[...] = new_s
        cst_ref[...] = new_c


def _mixer(x, shift, scale, g_pre, w_in, w_pool_bd, pool_scale, w_sconv, w_cconv, b_cconv, g_cnorm,
           b_cnorm, w_branch, cos, sin, pool_pre, sconv_pre, cconv_pre, *, layer, tm, stride, pos_base,
           kv_stack=None):
    depth = w_in.shape[0]
    nb, rows, _ = x.shape
    n_tiles = rows // tm
    hp = _halo_steps(POOL_BUF, stride) * stride
    hs = _halo_steps(SCONV_K - 1, stride) * stride
    hc = _halo_steps(CCONV_K - 1, stride) * stride
    mod_rows = shift.shape[1]
    once = pl.Buffered(1)

    def const(shape):
        nd = len(shape)
        return pl.BlockSpec(shape, lambda b, i: (0,) * nd, pipeline_mode=once)

    def per_layer(shape):
        nd = len(shape)
        return pl.BlockSpec((None,) + shape, lambda b, i: (layer,) + (0,) * nd, pipeline_mode=once)

    def per_b(shape):
        return pl.BlockSpec((None,) + shape, lambda b, i: (b, 0, 0))

    def tiled(width):
        return pl.BlockSpec((None, tm, width), lambda b, i: (b, i, 0))

    mod_spec = per_b((1, D_MODEL)) if mod_rows == 1 else tiled(D_MODEL)
    row_vec = lambda a: a.reshape(1, -1)
    stacked = kv_stack is not None
    n_alias = len(kv_stack) if stacked else 0
    kern = functools.partial(_mixer_kernel, tm=tm, stride=stride, pos_base=pos_base, n_tiles=n_tiles,
                             n_alias=n_alias, stacked=stacked)
    out_shapes = [jax.ShapeDtypeStruct((nb, rows, w), dt)
                  for w, dt in ((BRANCH_W, _BF16), (BRANCH_W, _BF16 if stacked else _F32), (BRANCH_W, _F32),
                                (D_MODEL, _BF16), (BRANCH_W, _BF16), (D_MODEL, _BF16))]
    out_shapes += [jax.ShapeDtypeStruct((nb, hh, BRANCH_W), _F32) for hh in (hp, hs, hc)]
    out_specs = [tiled(BRANCH_W), tiled(BRANCH_W), tiled(BRANCH_W), tiled(D_MODEL), tiled(BRANCH_W),
                 tiled(D_MODEL), per_b((hp, BRANCH_W)), per_b((hs, BRANCH_W)), per_b((hc, BRANCH_W))]
    aliases = {}
    if stacked:
        out_shapes[2] = jax.ShapeDtypeStruct((depth, nb, rows * N_HEADS, HEAD_DV), _F32)
        out_specs[2] = pl.BlockSpec((None, None, tm * N_HEADS, HEAD_DV), lambda b, i: (layer, b, i, 0))
        out_shapes.append(jax.ShapeDtypeStruct((depth, nb, BRANCH_W, rows), _F32))
        out_specs.append(pl.BlockSpec((None, None, BRANCH_W, tm), lambda b, i: (layer, b, 0, i)))
        if n_alias:
            aliases = {_MIXER_INPUTS: 2, _MIXER_INPUTS + 1: 9}
    return pl.pallas_call(
        kern,
        grid=(nb, n_tiles),
        in_specs=[
            tiled(D_MODEL), mod_spec, mod_spec, const((1, D_MODEL)),
            per_layer((D_MODEL, N_IN)), const((BRANCH_W, BRANCH_W)), const((1, BRANCH_W)),
            const((SCONV_K, BRANCH_W)), const((CCONV_K, BRANCH_W)), const((1, BRANCH_W)),
            const((1, BRANCH_W)), const((1, BRANCH_W)), per_layer((N_BRANCH, BRANCH_W, D_MODEL)),
            pl.BlockSpec((tm, _LANES), lambda b, i: (i, 0)), pl.BlockSpec((tm, _LANES), lambda b, i: (i, 0)),
            per_b((hp, BRANCH_W)), per_b((hs, BRANCH_W)), per_b((hc, BRANCH_W)),
        ] + [pl.BlockSpec(memory_space=pl.ANY)] * n_alias,
        out_specs=out_specs,
        out_shape=out_shapes,
        input_output_aliases=aliases,
        scratch_shapes=[pltpu.VMEM((hp + tm, BRANCH_W), _F32), pltpu.VMEM((hs + tm, BRANCH_W), _F32),
                        pltpu.VMEM((hc + tm, BRANCH_W), _F32),
                        pltpu.VMEM((BRANCH_W // _LANES, _SUBLANES - 1, tm + _SUBLANES, _LANES), _F32)],
        compiler_params=pltpu.CompilerParams(dimension_semantics=("arbitrary", "arbitrary"),
                                             vmem_limit_bytes=_VMEM_LIMIT),
        name="mixer_s%d" % stride,
    )(x, shift, scale, row_vec(g_pre), w_in, w_pool_bd, row_vec(pool_scale), w_sconv, w_cconv,
      row_vec(b_cconv), row_vec(g_cnorm), row_vec(b_cnorm), w_branch, cos, sin,
      pool_pre, sconv_pre, cconv_pre, *(kv_stack or ()))


def _attn_prompt_kernel(lq_ref, q_ref, k_ref, v_ref, o_ref, vt_ref, qs_ref, acc_ref, m_ref, l_ref, st_ref, *,
                        tq, n_sub, lam_init):
    i = pl.program_id(1)
    n_kv = k_ref.shape[0] // tq
    lam = _lambda(lq_ref, lam_init)

    @pl.when(i == 0)
    def _():
        for h in range(N_HEADS):
            for c in range(n_kv):
                vt_ref[h, c] = v_ref[pl.ds(c * tq * N_HEADS + h, tq, stride=N_HEADS), :].T.astype(_BF16)

    lane = lax.broadcasted_iota(jnp.int32, (tq, HEAD_DV), 1)
    first = lane < HEAD_DK
    for s in range(n_sub):
        for h in range(N_HEADS):
            qh = q_ref[s * tq:(s + 1) * tq, h * HEAD_DV:(h + 1) * HEAD_DV].astype(_F32)
            qs_ref[s * N_HEADS + h, 0:tq, :] = jnp.where(first, qh, 0.0).astype(_BF16)
            qs_ref[s * N_HEADS + h, tq:2 * tq, :] = jnp.where(first, 0.0, qh).astype(_BF16)
    acc_ref[...] = jnp.zeros_like(acc_ref)
    m_ref[...] = jnp.full(m_ref.shape, -jnp.inf, _F32)
    l_ref[...] = jnp.zeros_like(l_ref)
    causal = (lax.broadcasted_iota(jnp.int32, (tq, 2 * tq), 0)
              <= lax.broadcasted_iota(jnp.int32, (tq, 2 * tq), 1) % tq)

    def tile(j, subs):
        r0 = pl.multiple_of(j * tq, tq)
        for h in range(N_HEADS):
            kj = k_ref[pl.ds(r0, tq), h * HEAD_DV:(h + 1) * HEAD_DV].astype(_BF16)
            for s, _ in subs:
                st_ref[s * N_HEADS + h] = _dot_nt(kj, qs_ref[s * N_HEADS + h])
        for h in range(N_HEADS):
            vtj = vt_ref[h, j]
            for s, masked in subs:
                c = s * N_HEADS + h
                st = st_ref[c]
                if masked:
                    st = jnp.where(causal, st, -jnp.inf)
                m = m_ref[c:c + 1, :]
                m_new = jnp.maximum(m, jnp.max(st, axis=0, keepdims=True))
                p = jnp.exp2(st - m_new)
                alpha = jnp.exp2(m - m_new)
                m_ref[c:c + 1, :] = m_new
                l_ref[c:c + 1, :] = alpha * l_ref[c:c + 1, :] + jnp.sum(p, axis=0, keepdims=True)
                acc_ref[c] = alpha * acc_ref[c] + jnp.dot(vtj, p.astype(_BF16), preferred_element_type=_F32)

    def body(j, carry):
        tile(j, [(s, False) for s in range(n_sub)])
        return carry

    lax.fori_loop(0, n_sub * i, body, 0)
    for t in range(n_sub):
        tile(n_sub * i + t, [(t, True)] + [(s, False) for s in range(t + 1, n_sub)])
    for s in range(n_sub):
        for h in range(N_HEADS):
            c = s * N_HEADS + h
            a = acc_ref[c] * (1.0 / l_ref[c:c + 1, :])
            o_t = a[:, 0:tq] - lam * a[:, tq:2 * tq]
            o_ref[s * tq:(s + 1) * tq, h * HEAD_DV:(h + 1) * HEAD_DV] = o_t.T.astype(o_ref.dtype)


def _attn_prompt(lambda_qk_l, q, k, v_all, layer, lam_init):
    nb, s, _ = q.shape
    tq = _ATTN_TILE
    n_sub = _ATTN_SUBTILES
    n_chain = n_sub * N_HEADS
    kern = functools.partial(_attn_prompt_kernel, tq=tq, n_sub=n_sub, lam_init=lam_init)
    return pl.pallas_call(
        kern,
        grid=(nb, s // (n_sub * tq)),
        in_specs=[
            pl.BlockSpec((4, HEAD_DK), lambda b, i: (0, 0)),
            pl.BlockSpec((None, n_sub * tq, BRANCH_W), lambda b, i: (b, i, 0)),
            pl.BlockSpec((None, s, BRANCH_W), lambda b, i: (b, 0, 0)),
            pl.BlockSpec((None, None, s * N_HEADS, HEAD_DV), lambda b, i: (layer, b, 0, 0)),
        ],
        out_specs=pl.BlockSpec((None, n_sub * tq, BRANCH_W), lambda b, i: (b, i, 0)),
        out_shape=jax.ShapeDtypeStruct((nb, s, BRANCH_W), _BF16),
        scratch_shapes=[pltpu.VMEM((N_HEADS, s // tq, HEAD_DV, tq), _BF16),
                        pltpu.VMEM((n_chain, 2 * tq, HEAD_DV), _BF16),
                        pltpu.VMEM((n_chain, HEAD_DV, 2 * tq), _F32),
                        pltpu.VMEM((n_chain, 2 * tq), _F32), pltpu.VMEM((n_chain, 2 * tq), _F32),
                        pltpu.VMEM((n_chain, tq, 2 * tq), _F32)],
        compiler_params=pltpu.CompilerParams(dimension_semantics=("arbitrary", "arbitrary"),
                                             vmem_limit_bytes=_VMEM_LIMIT),
        name="attn_prompt",
    )(lambda_qk_l, q, k, v_all)


def _attn_sample_kernel(pt_ref, lq_ref, q_ref, kn_ref, vn_ref, *rest, n_pages, page, t_new, lam_init):
    del pt_ref
    k_pages = rest[:n_pages]
    v_pages = rest[n_pages:2 * n_pages]
    o_ref = rest[2 * n_pages]
    s_ref, knew, vnew = rest[2 * n_pages + 1:]
    lam = _lambda(lq_ref, lam_init)
    n_rows = 2 * N_HEADS * t_new
    half = N_HEADS * t_new

    r = lax.broadcasted_iota(jnp.int32, (n_rows, BRANCH_W), 0)
    c = lax.broadcasted_iota(jnp.int32, (n_rows, BRANCH_W), 1)
    q_rep = jnp.zeros((n_rows, BRANCH_W), _F32)
    for t in range(t_new):
        q_rep = jnp.where(r % t_new == t, q_ref[t:t + 1, :], q_rep)
    chunk_of_row = 2 * ((r % half) // t_new) + r // half
    wq = jnp.where(c // HEAD_DK == chunk_of_row, q_rep, 0.0).astype(_BF16)

    for p in range(n_pages):
        s_ref[:, p * page:(p + 1) * page] = _dot(wq, k_pages[p][...])

    @pl.when(pl.program_id(0) == 0)
    def _():
        knew[...] = jnp.zeros_like(knew)
        vnew[...] = jnp.zeros_like(vnew)

    knew[0:t_new, :] = kn_ref[...]
    vnew[0:t_new, :] = vn_ref[...]
    s_new = _dot_nt(wq, knew[...])
    key = lax.broadcasted_iota(jnp.int32, (n_rows, page), 1)
    tok = lax.broadcasted_iota(jnp.int32, (n_rows, page), 0) % t_new
    s_ref[:, n_pages * page:(n_pages + 1) * page] = jnp.where(key <= tok, s_new, -jnp.inf)

    s = s_ref[...]
    e = jnp.exp2(s - jnp.max(s, axis=-1, keepdims=True))
    pr = e / jnp.sum(e, axis=-1, keepdims=True)
    wts = (pr[0:half, :] - lam * pr[half:n_rows, :]).astype(_BF16)

    acc = jnp.dot(wts[:, n_pages * page:(n_pages + 1) * page], vnew[...].astype(_BF16),
                  preferred_element_type=_F32)
    for p in range(n_pages):
        v_cat = jnp.concatenate([v_pages[p][pl.ds(h, page, stride=N_HEADS), :] for h in range(N_HEADS)],
                                axis=-1).astype(_BF16)
        acc = acc + jnp.dot(wts[:, p * page:(p + 1) * page], v_cat, preferred_element_type=_F32)
    ar = lax.broadcasted_iota(jnp.int32, (half, BRANCH_W), 0)
    ac = lax.broadcasted_iota(jnp.int32, (half, BRANCH_W), 1)
    own = jnp.where(ar // t_new == ac // HEAD_DV, acc, 0.0)
    tot = own
    for h in range(1, N_HEADS):
        tot = tot + pltpu.roll(own, h * t_new, axis=0)
    o_ref[...] = tot[0:t_new, :]


def _attn_sample(layer, page_table, cache_kt, cache_v, lambda_qk_l, q, k_new, v_new, lam_init):
    nb, t_new, _ = q.shape
    n_pages = page_table.shape[1]
    page = cache_kt.shape[3]

    def k_spec(j):
        return pl.BlockSpec((None, None, BRANCH_W, page), lambda b, pt: (layer, pt[b, j], 0, 0))

    def v_spec(j):
        return pl.BlockSpec((None, None, N_HEADS * page, HEAD_DV), lambda b, pt: (layer, pt[b, j], 0, 0))

    tok_spec = pl.BlockSpec((None, t_new, BRANCH_W), lambda b, pt: (b, 0, 0))
    kern = functools.partial(_attn_sample_kernel, n_pages=n_pages, page=page, t_new=t_new, lam_init=lam_init)
    n_rows = 2 * N_HEADS * t_new
    grid_spec = pltpu.PrefetchScalarGridSpec(
        num_scalar_prefetch=1,
        grid=(nb,),
        in_specs=[pl.BlockSpec((4, HEAD_DK), lambda b, pt: (0, 0)), tok_spec, tok_spec, tok_spec]
        + [k_spec(j) for j in range(n_pages)] + [v_spec(j) for j in range(n_pages)],
        out_specs=tok_spec,
        scratch_shapes=[pltpu.VMEM((n_rows, (n_pages + 1) * page), _F32),
                        pltpu.VMEM((page, BRANCH_W), _F32), pltpu.VMEM((page, BRANCH_W), _F32)],
    )
    return pl.pallas_call(
        kern,
        grid_spec=grid_spec,
        out_shape=jax.ShapeDtypeStruct((nb, t_new, BRANCH_W), _F32),
        compiler_params=pltpu.CompilerParams(dimension_semantics=("arbitrary",),
                                             vmem_limit_bytes=_VMEM_LIMIT),
        name="attn_sample",
    )(page_table, lambda_qk_l, q, k_new, v_new, *([cache_kt] * n_pages), *([cache_v] * n_pages))


def _output_kernel(o_ref, gd_ref, sd_ref, mabc_ref, x_ref, gate_ref, gsub_ref, wbr_ref, wo_ref, gpost_ref,
                   y_ref, *, lam_init):
    o = o_ref[...].astype(_F32)
    parts = [_rms(o[:, h * HEAD_DV:(h + 1) * HEAD_DV], gsub_ref[...]) * (1.0 - lam_init)
             for h in range(N_HEADS)]
    ys = jnp.concatenate(parts, axis=-1) * gd_ref[...].astype(_F32)
    merged = mabc_ref[...].astype(_F32) + sd_ref[...].astype(_F32) * _dot(ys, wbr_ref[...])
    out = _rms(_dot(merged, wo_ref[...]), gpost_ref[...])
    y_ref[...] = x_ref[...] + gate_ref[...] * out


def _output(o, gd, sd, mabc, x, gate, g_subln, w_branch, w_o, g_post, *, layer, tm, lam_init):
    nb, rows, _ = x.shape
    once = pl.Buffered(1)

    def tiled(width):
        return pl.BlockSpec((None, tm, width), lambda b, i: (b, i, 0))

    gate_spec = (pl.BlockSpec((None, 1, D_MODEL), lambda b, i: (b, 0, 0)) if gate.shape[1] == 1
                 else tiled(D_MODEL))
    kern = functools.partial(_output_kernel, lam_init=lam_init)
    return pl.pallas_call(
        kern,
        grid=(nb, rows // tm),
        in_specs=[
            tiled(BRANCH_W), tiled(BRANCH_W), tiled(D_MODEL), tiled(D_MODEL), tiled(D_MODEL), gate_spec,
            pl.BlockSpec((1, HEAD_DV), lambda b, i: (0, 0)),
            pl.BlockSpec((None, None, BRANCH_W, D_MODEL), lambda b, i: (layer, N_BRANCH - 1, 0, 0),
                         pipeline_mode=once),
            pl.BlockSpec((None, D_MODEL, D_MODEL), lambda b, i: (layer, 0, 0), pipeline_mode=once),
            pl.BlockSpec((1, D_MODEL), lambda b, i: (0, 0)),
        ],
        out_specs=tiled(D_MODEL),
        out_shape=jax.ShapeDtypeStruct((nb, rows, D_MODEL), _F32),
        compiler_params=pltpu.CompilerParams(dimension_semantics=("arbitrary", "arbitrary"),
                                             vmem_limit_bytes=_VMEM_LIMIT),
        name="output_proj",
    )(o, gd, sd, mabc, x, gate, g_subln.reshape(1, -1), w_branch, w_o, g_post.reshape(1, -1))


def _to_time_major(a):
    nb, t, c = a.shape
    g = nb // _SAMPLE_GROUP
    return a.reshape(g, _SAMPLE_GROUP, t, c).transpose(0, 2, 1, 3).reshape(g, t * _SAMPLE_GROUP, c)


def _from_time_major(a, t):
    g, rows, c = a.shape
    return a.reshape(g, t, _SAMPLE_GROUP, c).transpose(0, 2, 1, 3).reshape(g * _SAMPLE_GROUP, t, c)


def _rope_tables(pos):
    half = HEAD_DK // 2
    inv = jnp.power(ROPE_THETA, -jnp.arange(half, dtype=_F32) / half)
    ang = pos.astype(_F32)[:, None] * inv[None, :]
    cos, sin = jnp.cos(ang), jnp.sin(ang)
    reps = _LANES // HEAD_DK
    cos_t = jnp.tile(jnp.concatenate([cos, cos], axis=-1), (1, reps))
    sin_t = jnp.tile(jnp.concatenate([-sin, sin], axis=-1), (1, reps))
    return cos_t, sin_t


def _block_diag(w):
    g, a, b = w.shape
    out = jnp.zeros((g * a, g * b), w.dtype)
    for n in range(g):
        out = out.at[n * a:(n + 1) * a, n * b:(n + 1) * b].set(w[n])
    return out


def kernel(x_prompt, x_sample, cache_k, cache_v, page_table, state_pool, state_sconv, state_cconv,
           c_prompt, c_sample, w_ada, b_ada, g_pre, g_post, w_in, w_pool, pool_scale, w_sconv, w_cconv,
           b_cconv, g_cnorm, b_cnorm, lambda_qk, g_subln, w_branch, w_o):
    bp, sp, _ = x_prompt.shape
    db, ds, _ = x_sample.shape
    depth = w_in.shape[0]
    n_pool, page = cache_k.shape[1], cache_k.shape[2]
    past = page_table.shape[1] * page
    grp = _SAMPLE_GROUP

    mod = _modulation(jnp.concatenate([c_prompt, c_sample], axis=0), w_ada, b_ada)
    cos_p, sin_p = _rope_tables(jnp.arange(sp))
    cos_s, sin_s = _rope_tables(jnp.repeat(past + jnp.arange(ds), grp))
    cache_kt = cache_k.transpose(0, 1, 3, 4, 5, 2).reshape(depth, n_pool, BRANCH_W, page)
    cache_v2 = cache_v.reshape(depth, n_pool, page * N_HEADS, HEAD_DV)

    hp1, hs1, hc1 = (_halo_steps(n, 1) for n in (POOL_BUF, SCONV_K - 1, CCONV_K - 1))
    zero_pre = [jnp.zeros((bp, hh, BRANCH_W), _F32) for hh in (hp1, hs1, hc1)]

    w_in_b, w_br_b, w_o_b = (w.astype(_BF16) for w in (w_in, w_branch, w_o))
    xp = x_prompt
    xs = _to_time_major(x_sample)
    outs_p = [[] for _ in range(3)]
    outs_s = [[] for _ in range(5)]
    kv_stack = ()
    for l in range(depth):
        lam_init = 0.8 - 0.6 * math.exp(-0.3 * l)
        w_pool_l = _block_diag(w_pool[l]).astype(_BF16)
        weights = (g_pre[l], w_in_b, w_pool_l, pool_scale[l], w_sconv[l], w_cconv[l], b_cconv[l],
                   g_cnorm[l], b_cnorm[l], w_br_b)

        mod_p = mod[l, :bp]
        shift, scale, gate = (mod_p[:, None, n * D_MODEL:(n + 1) * D_MODEL] for n in range(3))
        q, k, v_all, mabc, gd, sd, pst, sst, cst, kt_all = _mixer(
            xp, shift, scale, *weights, cos_p, sin_p, *zero_pre, layer=l, tm=_PROMPT_TILE, stride=1, pos_base=0,
            kv_stack=kv_stack)
        kv_stack = (v_all, kt_all)
        o = _attn_prompt(lambda_qk[l], q, k, v_all, l, lam_init)
        xp = _output(o, gd, sd, mabc, xp, gate, g_subln[l], w_br_b, w_o_b, g_post[l],
                     layer=l, tm=_OUTPUT_TILE, lam_init=lam_init)
        for lst, t in zip(outs_p, (pst[:, hp1 - POOL_BUF:], sst[:, hs1 - (SCONV_K - 1):],
                                   cst[:, hc1 - (CCONV_K - 1):])):
            lst.append(t)

        mod_s = mod[l, bp:]
        shift, scale, gate = (_to_time_major(jnp.broadcast_to(mod_s[:, None, n * D_MODEL:(n + 1) * D_MODEL],
                                                              (db, ds, D_MODEL))) for n in range(3))
        pre = [_to_time_major(st[l]) for st in (state_pool, state_sconv, state_cconv)]
        q, k, v, mabc, gd, sd, pst, sst, cst = _mixer(
            xs, shift, scale, *weights, cos_s, sin_s, *pre, layer=l, tm=ds * grp, stride=grp, pos_base=past)
        q_b, k_b, v_b = (_from_time_major(t, ds) for t in (q.astype(_F32), k, v))
        o_b = _attn_sample(l, page_table, cache_kt, cache_v2, lambda_qk[l], q_b, k_b, v_b, lam_init)
        xs = _output(_to_time_major(o_b.astype(_BF16)), gd, sd, mabc, xs, gate, g_subln[l], w_br_b, w_o_b, g_post[l],
                     layer=l, tm=ds * grp, lam_init=lam_init)
        for lst, t in zip(outs_s, (k_b.reshape(db, ds, N_HEADS, 2, HEAD_DK), v_b.reshape(db, ds, N_HEADS, HEAD_DV),
                                   _from_time_major(pst, POOL_BUF), _from_time_major(sst, SCONV_K - 1),
                                   _from_time_major(cst, CCONV_K - 1))):
            lst.append(t)

    y_sample = _from_time_major(xs, ds)
    v_all, kt_all = kv_stack
    k_prompt = kt_all.reshape(depth, bp, N_HEADS, 2, HEAD_DK, sp).transpose(0, 1, 5, 2, 3, 4)
    v_prompt = v_all.reshape(depth, bp, sp, N_HEADS, HEAD_DV)
    stacked_p = [jnp.stack(t) for t in outs_p]
    stacked_s = [jnp.stack(t) for t in outs_s]
    return (xp, y_sample, k_prompt, v_prompt, *stacked_p, *stacked_s)
```

```python
import functools
import math

import jax
import jax.numpy as jnp
from jax import lax
from jax.experimental import pallas as pl
from jax.experimental.pallas import tpu as pltpu

D_MODEL = 1024
N_BRANCH = 4
BRANCH_W = D_MODEL // 2
POOL_WINDOWS = (2, 4, 8, 16)
POOL_GC = BRANCH_W // len(POOL_WINDOWS)
POOL_BUF = max(POOL_WINDOWS) - 1
SCONV_K = 3
CCONV_K = 31
HEAD_DK = 64
HEAD_DV = 2 * HEAD_DK
N_HEADS = BRANCH_W // HEAD_DV
ATTN_SCALE = HEAD_DK ** -0.5
ROPE_THETA = 10000.0
EPS = 1e-6

_C_POOL = 0
_C_BG = BRANCH_W
_C_CG = 2 * BRANCH_W
_C_HB = 3 * BRANCH_W
_C_VAL = 4 * BRANCH_W
_C_GL = 5 * BRANCH_W
_C_Q = 6 * BRANCH_W
_C_K = 7 * BRANCH_W
_C_V = 8 * BRANCH_W
_C_MGATE = 9 * BRANCH_W
_C_MRG = _C_MGATE + N_BRANCH * BRANCH_W
N_IN = _C_MRG + N_BRANCH * D_MODEL

_LANES = 128
_SUBLANES = 8
_VMEM_LIMIT = 56 * 1024 * 1024

_PROMPT_TILE = 256
_OUTPUT_TILE = 1024
_ATTN_TILE = 256
_ATTN_SUBTILES = 4
_SAMPLE_GROUP = 32

_BF16 = jnp.bfloat16
_F32 = jnp.float32
_LOG2E = math.log2(math.e)


def _rms(x, g):
    return x * lax.rsqrt(jnp.mean(x * x, axis=-1, keepdims=True) + EPS) * g


def _sigmoid(x):
    return 0.5 * jnp.tanh(0.5 * x) + 0.5


def _silu(x):
    return x * _sigmoid(x)


def _dot(a, b):
    return jnp.dot(a.astype(_BF16), b.astype(_BF16), preferred_element_type=_F32)


def _dot_nt(a, b):
    return lax.dot_general(a.astype(_BF16), b.astype(_BF16), (((1,), (1,)), ((), ())),
                           preferred_element_type=_F32)


def _lambda(lq_ref, lam_init):
    lq = lq_ref[...]
    a = jnp.sum(lq[0:1, :] * lq[1:2, :], axis=-1, keepdims=True)
    b = jnp.sum(lq[2:3, :] * lq[3:4, :], axis=-1, keepdims=True)
    return jnp.exp(a) - jnp.exp(b) + lam_init


def _mod_kernel(c_ref, w_ref, b_ref, o_ref):
    o_ref[...] = _dot(_silu(c_ref[...]), w_ref[...]) + b_ref[...]


def _modulation(c_all, w_ada, b_ada):
    depth = w_ada.shape[0]
    rows = c_all.shape[0]
    n_blk = 3
    return pl.pallas_call(
        _mod_kernel,
        grid=(depth, n_blk),
        in_specs=[
            pl.BlockSpec((rows, D_MODEL), lambda l, j: (0, 0)),
            pl.BlockSpec((None, D_MODEL, D_MODEL), lambda l, j: (l, 0, j)),
            pl.BlockSpec((None, 1, D_MODEL), lambda l, j: (l, 0, j)),
        ],
        out_specs=pl.BlockSpec((None, rows, D_MODEL), lambda l, j: (l, 0, j)),
        out_shape=jax.ShapeDtypeStruct((depth, rows, 3 * D_MODEL), _F32),
        compiler_params=pltpu.CompilerParams(dimension_semantics=("arbitrary", "arbitrary"),
                                             vmem_limit_bytes=_VMEM_LIMIT),
        name="adaln_mod",
    )(c_all, w_ada, b_ada.reshape(depth, 1, 3 * D_MODEL))


def _halo_steps(k_minus_1, stride):
    if stride == 1:
        return -(-k_minus_1 // _SUBLANES) * _SUBLANES
    return k_minus_1


_MIXER_INPUTS = 18


def _mixer_kernel(*refs, tm, stride, pos_base, n_tiles, n_alias, stacked):
    (x_ref, shift_ref, scale_ref, gpre_ref, win_ref, wpool_ref, pscale_ref, wsc_ref, wcc_ref, bcc_ref,
     gcn_ref, bcn_ref, wbr_ref, cos_ref, sin_ref, ppre_ref, spre_ref, cpre_ref) = refs[:_MIXER_INPUTS]
    outs = refs[_MIXER_INPUTS + n_alias:]
    q_ref, k_ref, v_ref, mabc_ref, gd_ref, sd_ref, pst_ref, sst_ref, cst_ref = outs[:9]
    n_out = 10 if stacked else 9
    extp, exts, extc, hbuf = outs[n_out:]
    i = pl.program_id(1)
    hp = _halo_steps(POOL_BUF, stride) * stride
    hs = _halo_steps(SCONV_K - 1, stride) * stride
    hc = _halo_steps(CCONV_K - 1, stride) * stride

    @pl.when(i == 0)
    def _():
        extp[0:hp, :] = ppre_ref[...]
        exts[0:hs, :] = spre_ref[...]
        extc[0:hc, :] = cpre_ref[...]

    x = x_ref[...]
    h = _rms(x, gpre_ref[...]) * (1.0 + scale_ref[...]) + shift_ref[...]
    hb = h.astype(_BF16)

    def proj(c0, n):
        return jnp.dot(hb, win_ref[:, c0:c0 + n], preferred_element_type=_F32)

    def gated_branch(y, n):
        ys = y * _silu(proj(_C_MGATE + n * BRANCH_W, BRANCH_W))
        br = _dot(ys, wbr_ref[n])
        return _sigmoid(proj(_C_MRG + n * D_MODEL, D_MODEL)) * br

    u_a = proj(_C_POOL, BRANCH_W)
    extp[hp:hp + tm, :] = u_a
    row = lax.broadcasted_iota(jnp.int32, (tm, 1), 0) + i * tm
    step = row if stride == 1 else lax.shift_right_logical(row, int(math.log2(stride)))
    pos = pos_base + step
    d_parts = []
    for g, win in enumerate(POOL_WINDOWS):
        sl = slice(g * POOL_GC, (g + 1) * POOL_GC)
        cur = extp[hp:hp + tm, sl]
        acc = cur
        for d in range(1, win):
            acc = acc + extp[hp - d * stride:hp - d * stride + tm, sl]
        cnt = jnp.minimum(win, pos + 1).astype(_F32)
        d_parts.append(acc / cnt - cur)
    d_pool = jnp.concatenate(d_parts, axis=-1)
    y_a = _dot(d_pool, wpool_ref[...]) * pscale_ref[...]
    merged = gated_branch(y_a, 0)

    z_b = proj(_C_CG, BRANCH_W) * proj(_C_HB, BRANCH_W)
    exts[hs:hs + tm, :] = z_b
    conv_b = wsc_ref[SCONV_K - 1:SCONV_K, :] * z_b
    for j in range(SCONV_K - 1):
        off = hs - (SCONV_K - 1 - j) * stride
        conv_b = conv_b + wsc_ref[j:j + 1, :] * exts[off:off + tm, :]
    y_b = proj(_C_BG, BRANCH_W) * conv_b
    merged = merged + gated_branch(y_b, 1)

    z_c = proj(_C_VAL, BRANCH_W) * _sigmoid(proj(_C_GL, BRANCH_W))
    extc[hc:hc + tm, :] = z_c
    groups = {}
    for j in range(CCONV_K):
        off = hc - (CCONV_K - 1 - j) * stride
        groups.setdefault(off % _SUBLANES, []).append((j, off - off % _SUBLANES))
    conv_parts = []
    for ci, c0 in enumerate(range(0, BRANCH_W, _LANES)):
        sl = slice(c0, c0 + _LANES)
        out = None
        for rem, taps in sorted(groups.items()):
            n = tm if rem == 0 else tm + _SUBLANES
            acc = None
            for j, base in taps:
                term = wcc_ref[j:j + 1, sl] * extc[base:base + n, sl]
                acc = term if acc is None else acc + term
            if rem:
                hbuf[ci, rem - 1] = acc
                acc = hbuf[ci, rem - 1, rem:rem + tm, :]
            out = acc if out is None else out + acc
        conv_parts.append(out)
    conv_c = jnp.concatenate(conv_parts, axis=-1) + bcc_ref[...]
    mu = jnp.mean(conv_c, axis=-1, keepdims=True)
    xc = conv_c - mu
    ln = xc * lax.rsqrt(jnp.mean(xc * xc, axis=-1, keepdims=True) + EPS) * gcn_ref[...] + bcn_ref[...]
    y_c = _silu(ln)
    merged = merged + gated_branch(y_c, 2)
    mabc_ref[...] = merged.astype(mabc_ref.dtype)

    cos = jnp.concatenate([cos_ref[...]] * (BRANCH_W // _LANES), axis=-1)
    sin = jnp.concatenate([sin_ref[...]] * (BRANCH_W // _LANES), axis=-1)
    lane = lax.broadcasted_iota(jnp.int32, (tm, BRANCH_W), 1)
    first_half = (lane % HEAD_DK) < (HEAD_DK // 2)

    def rope(t):
        fwd = pltpu.roll(t, BRANCH_W - HEAD_DK // 2, axis=1)
        bwd = pltpu.roll(t, HEAD_DK // 2, axis=1)
        return t * cos + jnp.where(first_half, fwd, bwd) * sin

    q_ref[...] = (rope(proj(_C_Q, BRANCH_W)) * (ATTN_SCALE * _LOG2E)).astype(q_ref.dtype)
    k_rot = rope(proj(_C_K, BRANCH_W))
    k_ref[...] = k_rot.astype(k_ref.dtype)
    v_new = proj(_C_V, BRANCH_W)
    if stacked:
        outs[9][...] = k_rot.T
        for hd in range(N_HEADS):
            v_ref[pl.ds(hd, tm, stride=N_HEADS), :] = v_new[:, hd * HEAD_DV:(hd + 1) * HEAD_DV]
    else:
        v_ref[...] = v_new
    gd_ref[...] = _silu(proj(_C_MGATE + 3 * BRANCH_W, BRANCH_W)).astype(gd_ref.dtype)
    sd_ref[...] = _sigmoid(proj(_C_MRG + 3 * D_MODEL, D_MODEL)).astype(sd_ref.dtype)

    new_p = extp[tm:tm + hp, :]
    new_s = exts[tm:tm + hs, :]
    new_c = extc[tm:tm + hc, :]
    if n_tiles > 1:
        extp[0:hp, :] = new_p
        exts[0:hs, :] = new_s
        extc[0:hc, :] = new_c

    @pl.when(i == n_tiles - 1)
    def _():
        pst_ref[...] = new_p
        sst_ref[...] = new_s
        cst_ref[...] = new_c


def _mixer(x, shift, scale, g_pre, w_in, w_pool_bd, pool_scale, w_sconv, w_cconv, b_cconv, g_cnorm,
           b_cnorm, w_branch, cos, sin, pool_pre, sconv_pre, cconv_pre, *, layer, tm, stride, pos_base,
           kv_stack=None):
    depth = w_in.shape[0]
    nb, rows, _ = x.shape
    n_tiles = rows // tm
    hp = _halo_steps(POOL_BUF, stride) * stride
    hs = _halo_steps(SCONV_K - 1, stride) * stride
    hc = _halo_steps(CCONV_K - 1, stride) * stride
    mod_rows = shift.shape[1]
    once = pl.Buffered(1)

    def const(shape):
        nd = len(shape)
        return pl.BlockSpec(shape, lambda b, i: (0,) * nd, pipeline_mode=once)

    def per_layer(shape):
        nd = len(shape)
        return pl.BlockSpec((None,) + shape, lambda b, i: (layer,) + (0,) * nd, pipeline_mode=once)

    def per_b(shape):
        return pl.BlockSpec((None,) + shape, lambda b, i: (b, 0, 0))

    def tiled(width):
        return pl.BlockSpec((None, tm, width), lambda b, i: (b, i, 0))

    mod_spec = per_b((1, D_MODEL)) if mod_rows == 1 else tiled(D_MODEL)
    row_vec = lambda a: a.reshape(1, -1)
    stacked = kv_stack is not None
    n_alias = len(kv_stack) if stacked else 0
    kern = functools.partial(_mixer_kernel, tm=tm, stride=stride, pos_base=pos_base, n_tiles=n_tiles,
                             n_alias=n_alias, stacked=stacked)
    out_shapes = [jax.ShapeDtypeStruct((nb, rows, w), dt)
                  for w, dt in ((BRANCH_W, _BF16), (BRANCH_W, _BF16 if stacked else _F32), (BRANCH_W, _F32),
                                (D_MODEL, _BF16), (BRANCH_W, _BF16), (D_MODEL, _BF16))]
    out_shapes += [jax.ShapeDtypeStruct((nb, hh, BRANCH_W), _F32) for hh in (hp, hs, hc)]
    out_specs = [tiled(BRANCH_W), tiled(BRANCH_W), tiled(BRANCH_W), tiled(D_MODEL), tiled(BRANCH_W),
                 tiled(D_MODEL), per_b((hp, BRANCH_W)), per_b((hs, BRANCH_W)), per_b((hc, BRANCH_W))]
    aliases = {}
    if stacked:
        out_shapes[2] = jax.ShapeDtypeStruct((depth, nb, rows * N_HEADS, HEAD_DV), _F32)
        out_specs[2] = pl.BlockSpec((None, None, tm * N_HEADS, HEAD_DV), lambda b, i: (layer, b, i, 0))
        out_shapes.append(jax.ShapeDtypeStruct((depth, nb, BRANCH_W, rows), _F32))
        out_specs.append(pl.BlockSpec((None, None, BRANCH_W, tm), lambda b, i: (layer, b, 0, i)))
        if n_alias:
            aliases = {_MIXER_INPUTS: 2, _MIXER_INPUTS + 1: 9}
    return pl.pallas_call(
        kern,
        grid=(nb, n_tiles),
        in_specs=[
            tiled(D_MODEL), mod_spec, mod_spec, const((1, D_MODEL)),
            per_layer((D_MODEL, N_IN)), const((BRANCH_W, BRANCH_W)), const((1, BRANCH_W)),
            const((SCONV_K, BRANCH_W)), const((CCONV_K, BRANCH_W)), const((1, BRANCH_W)),
            const((1, BRANCH_W)), const((1, BRANCH_W)), per_layer((N_BRANCH, BRANCH_W, D_MODEL)),
            pl.BlockSpec((tm, _LANES), lambda b, i: (i, 0)), pl.BlockSpec((tm, _LANES), lambda b, i: (i, 0)),
            per_b((hp, BRANCH_W)), per_b((hs, BRANCH_W)), per_b((hc, BRANCH_W)),
        ] + [pl.BlockSpec(memory_space=pl.ANY)] * n_alias,
        out_specs=out_specs,
        out_shape=out_shapes,
        input_output_aliases=aliases,
        scratch_shapes=[pltpu.VMEM((hp + tm, BRANCH_W), _F32), pltpu.VMEM((hs + tm, BRANCH_W), _F32),
                        pltpu.VMEM((hc + tm, BRANCH_W), _F32),
                        pltpu.VMEM((BRANCH_W // _LANES, _SUBLANES - 1, tm + _SUBLANES, _LANES), _F32)],
        compiler_params=pltpu.CompilerParams(dimension_semantics=("arbitrary", "arbitrary"),
                                             vmem_limit_bytes=_VMEM_LIMIT),
        name="mixer_s%d" % stride,
    )(x, shift, scale, row_vec(g_pre), w_in, w_pool_bd, row_vec(pool_scale), w_sconv, w_cconv,
      row_vec(b_cconv), row_vec(g_cnorm), row_vec(b_cnorm), w_branch, cos, sin,
      pool_pre, sconv_pre, cconv_pre, *(kv_stack or ()))


def _attn_prompt_kernel(lq_ref, q_ref, k_ref, v_ref, o_ref, vt_ref, qs_ref, acc_ref, m_ref, l_ref, st_ref, *,
                        tq, n_sub, lam_init):
    i = pl.program_id(1)
    n_kv = k_ref.shape[0] // tq
    lam = _lambda(lq_ref, lam_init)

    @pl.when(i == 0)
    def _():
        for h in range(N_HEADS):
            for c in range(n_kv):
                vt_ref[h, c] = v_ref[pl.ds(c * tq * N_HEADS + h, tq, stride=N_HEADS), :].T.astype(_BF16)

    lane = lax.broadcasted_iota(jnp.int32, (tq, HEAD_DV), 1)
    first = lane < HEAD_DK
    for s in range(n_sub):
        for h in range(N_HEADS):
            qh = q_ref[s * tq:(s + 1) * tq, h * HEAD_DV:(h + 1) * HEAD_DV].astype(_F32)
            qs_ref[s * N_HEADS + h, 0:tq, :] = jnp.where(first, qh, 0.0).astype(_BF16)
            qs_ref[s * N_HEADS + h, tq:2 * tq, :] = jnp.where(first, 0.0, qh).astype(_BF16)
    acc_ref[...] = jnp.zeros_like(acc_ref)
    m_ref[...] = jnp.full(m_ref.shape, -jnp.inf, _F32)
    l_ref[...] = jnp.zeros_like(l_ref)
    causal = (lax.broadcasted_iota(jnp.int32, (tq, 2 * tq), 0)
              <= lax.broadcasted_iota(jnp.int32, (tq, 2 * tq), 1) % tq)

    def tile(j, subs):
        r0 = pl.multiple_of(j * tq, tq)
        for h in range(N_HEADS):
            kj = k_ref[pl.ds(r0, tq), h * HEAD_DV:(h + 1) * HEAD_DV].astype(_BF16)
            for s, _ in subs:
                st_ref[s * N_HEADS + h] = _dot_nt(kj, qs_ref[s * N_HEADS + h])
        for h in range(N_HEADS):
            vtj = vt_ref[h, j]
            for s, masked in subs:
                c = s * N_HEADS + h
                st = st_ref[c]
                if masked:
                    st = jnp.where(causal, st, -jnp.inf)
                m = m_ref[c:c + 1, :]
                m_new = jnp.maximum(m, jnp.max(st, axis=0, keepdims=True))
                p = jnp.exp2(st - m_new)
                alpha = jnp.exp2(m - m_new)
                m_ref[c:c + 1, :] = m_new
                l_ref[c:c + 1, :] = alpha * l_ref[c:c + 1, :] + jnp.sum(p, axis=0, keepdims=True)
                acc_ref[c] = alpha * acc_ref[c] + jnp.dot(vtj, p.astype(_BF16), preferred_element_type=_F32)

    def body(j, carry):
        tile(j, [(s, False) for s in range(n_sub)])
        return carry

    lax.fori_loop(0, n_sub * i, body, 0)
    for t in range(n_sub):
        tile(n_sub * i + t, [(t, True)] + [(s, False) for s in range(t + 1, n_sub)])
    for s in range(n_sub):
        for h in range(N_HEADS):
            c = s * N_HEADS + h
            a = acc_ref[c] * (1.0 / l_ref[c:c + 1, :])
            o_t = a[:, 0:tq] - lam * a[:, tq:2 * tq]
            o_ref[s * tq:(s + 1) * tq, h * HEAD_DV:(h + 1) * HEAD_DV] = o_t.T.astype(o_ref.dtype)


def _attn_prompt(lambda_qk_l, q, k, v_all, layer, lam_init):
    nb, s, _ = q.shape
    tq = _ATTN_TILE
    n_sub = _ATTN_SUBTILES
    n_chain = n_sub * N_HEADS
    kern = functools.partial(_attn_prompt_kernel, tq=tq, n_sub=n_sub, lam_init=lam_init)
    return pl.pallas_call(
        kern,
        grid=(nb, s // (n_sub * tq)),
        in_specs=[
            pl.BlockSpec((4, HEAD_DK), lambda b, i: (0, 0)),
            pl.BlockSpec((None, n_sub * tq, BRANCH_W), lambda b, i: (b, i, 0)),
            pl.BlockSpec((None, s, BRANCH_W), lambda b, i: (b, 0, 0)),
            pl.BlockSpec((None, None, s * N_HEADS, HEAD_DV), lambda b, i: (layer, b, 0, 0)),
        ],
        out_specs=pl.BlockSpec((None, n_sub * tq, BRANCH_W), lambda b, i: (b, i, 0)),
        out_shape=jax.ShapeDtypeStruct((nb, s, BRANCH_W), _BF16),
        scratch_shapes=[pltpu.VMEM((N_HEADS, s // tq, HEAD_DV, tq), _BF16),
                        pltpu.VMEM((n_chain, 2 * tq, HEAD_DV), _BF16),
                        pltpu.VMEM((n_chain, HEAD_DV, 2 * tq), _F32),
                        pltpu.VMEM((n_chain, 2 * tq), _F32), pltpu.VMEM((n_chain, 2 * tq), _F32),
                        pltpu.VMEM((n_chain, tq, 2 * tq), _F32)],
        compiler_params=pltpu.CompilerParams(dimension_semantics=("arbitrary", "arbitrary"),
                                             vmem_limit_bytes=_VMEM_LIMIT),
        name="attn_prompt",
    )(lambda_qk_l, q, k, v_all)


def _attn_sample_kernel(pt_ref, lq_ref, q_ref, kn_ref, vn_ref, *rest, n_pages, page, t_new, lam_init):
    del pt_ref
    k_pages = rest[:n_pages]
    v_pages = rest[n_pages:2 * n_pages]
    o_ref = rest[2 * n_pages]
    s_ref, knew, vnew = rest[2 * n_pages + 1:]
    lam = _lambda(lq_ref, lam_init)
    n_rows = 2 * N_HEADS * t_new
    half = N_HEADS * t_new

    r = lax.broadcasted_iota(jnp.int32, (n_rows, BRANCH_W), 0)
    c = lax.broadcasted_iota(jnp.int32, (n_rows, BRANCH_W), 1)
    q_rep = jnp.zeros((n_rows, BRANCH_W), _F32)
    for t in range(t_new):
        q_rep = jnp.where(r % t_new == t, q_ref[t:t + 1, :], q_rep)
    chunk_of_row = 2 * ((r % half) // t_new) + r // half
    wq = jnp.where(c // HEAD_DK == chunk_of_row, q_rep, 0.0).astype(_BF16)

    for p in range(n_pages):
        s_ref[:, p * page:(p + 1) * page] = _dot(wq, k_pages[p][...])

    @pl.when(pl.program_id(0) == 0)
    def _():
        knew[...] = jnp.zeros_like(knew)
        vnew[...] = jnp.zeros_like(vnew)

    knew[0:t_new, :] = kn_ref[...]
    vnew[0:t_new, :] = vn_ref[...]
    s_new = _dot_nt(wq, knew[...])
    key = lax.broadcasted_iota(jnp.int32, (n_rows, page), 1)
    tok = lax.broadcasted_iota(jnp.int32, (n_rows, page), 0) % t_new
    s_ref[:, n_pages * page:(n_pages + 1) * page] = jnp.where(key <= tok, s_new, -jnp.inf)

    s = s_ref[...]
    e = jnp.exp2(s - jnp.max(s, axis=-1, keepdims=True))
    pr = e / jnp.sum(e, axis=-1, keepdims=True)
    wts = (pr[0:half, :] - lam * pr[half:n_rows, :]).astype(_BF16)

    acc = jnp.dot(wts[:, n_pages * page:(n_pages + 1) * page], vnew[...].astype(_BF16),
                  preferred_element_type=_F32)
    for p in range(n_pages):
        v_cat = jnp.concatenate([v_pages[p][pl.ds(h, page, stride=N_HEADS), :] for h in range(N_HEADS)],
                                axis=-1).astype(_BF16)
        acc = acc + jnp.dot(wts[:, p * page:(p + 1) * page], v_cat, preferred_element_type=_F32)
    ar = lax.broadcasted_iota(jnp.int32, (half, BRANCH_W), 0)
    ac = lax.broadcasted_iota(jnp.int32, (half, BRANCH_W), 1)
    own = jnp.where(ar // t_new == ac // HEAD_DV, acc, 0.0)
    tot = own
    for h in range(1, N_HEADS):
        tot = tot + pltpu.roll(own, h * t_new, axis=0)
    o_ref[...] = tot[0:t_new, :]


def _attn_sample(layer, page_table, cache_kt, cache_v, lambda_qk_l, q, k_new, v_new, lam_init):
    nb, t_new, _ = q.shape
    n_pages = page_table.shape[1]
    page = cache_kt.shape[3]

    def k_spec(j):
        return pl.BlockSpec((None, None, BRANCH_W, page), lambda b, pt: (layer, pt[b, j], 0, 0))

    def v_spec(j):
        return pl.BlockSpec((None, None, N_HEADS * page, HEAD_DV), lambda b, pt: (layer, pt[b, j], 0, 0))

    tok_spec = pl.BlockSpec((None, t_new, BRANCH_W), lambda b, pt: (b, 0, 0))
    kern = functools.partial(_attn_sample_kernel, n_pages=n_pages, page=page, t_new=t_new, lam_init=lam_init)
    n_rows = 2 * N_HEADS * t_new
    grid_spec = pltpu.PrefetchScalarGridSpec(
        num_scalar_prefetch=1,
        grid=(nb,),
        in_specs=[pl.BlockSpec((4, HEAD_DK), lambda b, pt: (0, 0)), tok_spec, tok_spec, tok_spec]
        + [k_spec(j) for j in range(n_pages)] + [v_spec(j) for j in range(n_pages)],
        out_specs=tok_spec,
        scratch_shapes=[pltpu.VMEM((n_rows, (n_pages + 1) * page), _F32),
                        pltpu.VMEM((page, BRANCH_W), _F32), pltpu.VMEM((page, BRANCH_W), _F32)],
    )
    return pl.pallas_call(
        kern,
        grid_spec=grid_spec,
        out_shape=jax.ShapeDtypeStruct((nb, t_new, BRANCH_W), _F32),
        compiler_params=pltpu.CompilerParams(dimension_semantics=("arbitrary",),
                                             vmem_limit_bytes=_VMEM_LIMIT),
        name="attn_sample",
    )(page_table, lambda_qk_l, q, k_new, v_new, *([cache_kt] * n_pages), *([cache_v] * n_pages))


def _output_kernel(o_ref, gd_ref, sd_ref, mabc_ref, x_ref, gate_ref, gsub_ref, wbr_ref, wo_ref, gpost_ref,
                   y_ref, *, lam_init):
    o = o_ref[...].astype(_F32)
    parts = [_rms(o[:, h * HEAD_DV:(h + 1) * HEAD_DV], gsub_ref[...]) * (1.0 - lam_init)
             for h in range(N_HEADS)]
    ys = jnp.concatenate(parts, axis=-1) * gd_ref[...].astype(_F32)
    merged = mabc_ref[...].astype(_F32) + sd_ref[...].astype(_F32) * _dot(ys, wbr_ref[...])
    out = _rms(_dot(merged, wo_ref[...]), gpost_ref[...])
    y_ref[...] = x_ref[...] + gate_ref[...] * out


def _output(o, gd, sd, mabc, x, gate, g_subln, w_branch, w_o, g_post, *, layer, tm, lam_init):
    nb, rows, _ = x.shape
    once = pl.Buffered(1)

    def tiled(width):
        return pl.BlockSpec((None, tm, width), lambda b, i: (b, i, 0))

    gate_spec = (pl.BlockSpec((None, 1, D_MODEL), lambda b, i: (b, 0, 0)) if gate.shape[1] == 1
                 else tiled(D_MODEL))
    kern = functools.partial(_output_kernel, lam_init=lam_init)
    return pl.pallas_call(
        kern,
        grid=(nb, rows // tm),
        in_specs=[
            tiled(BRANCH_W), tiled(BRANCH_W), tiled(D_MODEL), tiled(D_MODEL), tiled(D_MODEL), gate_spec,
            pl.BlockSpec((1, HEAD_DV), lambda b, i: (0, 0)),
            pl.BlockSpec((None, None, BRANCH_W, D_MODEL), lambda b, i: (layer, N_BRANCH - 1, 0, 0),
                         pipeline_mode=once),
            pl.BlockSpec((None, D_MODEL, D_MODEL), lambda b, i: (layer, 0, 0), pipeline_mode=once),
            pl.BlockSpec((1, D_MODEL), lambda b, i: (0, 0)),
        ],
        out_specs=tiled(D_MODEL),
        out_shape=jax.ShapeDtypeStruct((nb, rows, D_MODEL), _F32),
        compiler_params=pltpu.CompilerParams(dimension_semantics=("arbitrary", "arbitrary"),
                                             vmem_limit_bytes=_VMEM_LIMIT),
        name="output_proj",
    )(o, gd, sd, mabc, x, gate, g_subln.reshape(1, -1), w_branch, w_o, g_post.reshape(1, -1))


def _to_time_major(a):
    nb, t, c = a.shape
    g = nb // _SAMPLE_GROUP
    return a.reshape(g, _SAMPLE_GROUP, t, c).transpose(0, 2, 1, 3).reshape(g, t * _SAMPLE_GROUP, c)


def _from_time_major(a, t):
    g, rows, c = a.shape
    return a.reshape(g, t, _SAMPLE_GROUP, c).transpose(0, 2, 1, 3).reshape(g * _SAMPLE_GROUP, t, c)


def _rope_tables(pos):
    half = HEAD_DK // 2
    inv = jnp.power(ROPE_THETA, -jnp.arange(half, dtype=_F32) / half)
    ang = pos.astype(_F32)[:, None] * inv[None, :]
    cos, sin = jnp.cos(ang), jnp.sin(ang)
    reps = _LANES // HEAD_DK
    cos_t = jnp.tile(jnp.concatenate([cos, cos], axis=-1), (1, reps))
    sin_t = jnp.tile(jnp.concatenate([-sin, sin], axis=-1), (1, reps))
    return cos_t, sin_t


def _block_diag(w):
    g, a, b = w.shape
    out = jnp.zeros((g * a, g * b), w.dtype)
    for n in range(g):
        out = out.at[n * a:(n + 1) * a, n * b:(n + 1) * b].set(w[n])
    return out


def kernel(x_prompt, x_sample, cache_k, cache_v, page_table, state_pool, state_sconv, state_cconv,
           c_prompt, c_sample, w_ada, b_ada, g_pre, g_post, w_in, w_pool, pool_scale, w_sconv, w_cconv,
           b_cconv, g_cnorm, b_cnorm, lambda_qk, g_subln, w_branch, w_o):
    bp, sp, _ = x_prompt.shape
    db, ds, _ = x_sample.shape
    depth = w_in.shape[0]
    n_pool, page = cache_k.shape[1], cache_k.shape[2]
    past = page_table.shape[1] * page
    grp = _SAMPLE_GROUP

    mod = _modulation(jnp.concatenate([c_prompt, c_sample], axis=0), w_ada, b_ada)
    cos_p, sin_p = _rope_tables(jnp.arange(sp))
    cos_s, sin_s = _rope_tables(jnp.repeat(past + jnp.arange(ds), grp))
    cache_kt = cache_k.transpose(0, 1, 3, 4, 5, 2).reshape(depth, n_pool, BRANCH_W, page)
    cache_v2 = cache_v.reshape(depth, n_pool, page * N_HEADS, HEAD_DV)

    hp1, hs1, hc1 = (_halo_steps(n, 1) for n in (POOL_BUF, SCONV_K - 1, CCONV_K - 1))
    zero_pre = [jnp.zeros((bp, hh, BRANCH_W), _F32) for hh in (hp1, hs1, hc1)]

    w_in_b, w_br_b, w_o_b = (w.astype(_BF16) for w in (w_in, w_branch, w_o))
    xp = x_prompt
    xs = _to_time_major(x_sample)
    outs_p = [[] for _ in range(3)]
    outs_s = [[] for _ in range(5)]
    kv_stack = ()
    for l in range(depth):
        lam_init = 0.8 - 0.6 * math.exp(-0.3 * l)
        w_pool_l = _block_diag(w_pool[l]).astype(_BF16)
        weights = (g_pre[l], w_in_b, w_pool_l, pool_scale[l], w_sconv[l], w_cconv[l], b_cconv[l],
                   g_cnorm[l], b_cnorm[l], w_br_b)

        mod_p = mod[l, :bp]
        shift, scale, gate = (mod_p[:, None, n * D_MODEL:(n + 1) * D_MODEL] for n in range(3))
        q, k, v_all, mabc, gd, sd, pst, sst, cst, kt_all = _mixer(
            xp, shift, scale, *weights, cos_p, sin_p, *zero_pre, layer=l, tm=_PROMPT_TILE, stride=1, pos_base=0,
            kv_stack=kv_stack)
        kv_stack = (v_all, kt_all)
        o = _attn_prompt(lambda_qk[l], q, k, v_all, l, lam_init)
        xp = _output(o, gd, sd, mabc, xp, gate, g_subln[l], w_br_b, w_o_b, g_post[l],
                     layer=l, tm=_OUTPUT_TILE, lam_init=lam_init)
        for lst, t in zip(outs_p, (pst[:, hp1 - POOL_BUF:], sst[:, hs1 - (SCONV_K - 1):],
                                   cst[:, hc1 - (CCONV_K - 1):])):
            lst.append(t)

        mod_s = mod[l, bp:]
        shift, scale, gate = (_to_time_major(jnp.broadcast_to(mod_s[:, None, n * D_MODEL:(n + 1) * D_MODEL],
                                                              (db, ds, D_MODEL))) for n in range(3))
        pre = [_to_time_major(st[l]) for st in (state_pool, state_sconv, state_cconv)]
        q, k, v, mabc, gd, sd, pst, sst, cst = _mixer(
            xs, shift, scale, *weights, cos_s, sin_s, *pre, layer=l, tm=ds * grp, stride=grp, pos_base=past)
        q_b, k_b, v_b = (_from_time_major(t, ds) for t in (q.astype(_F32), k, v))
        o_b = _attn_sample(l, page_table, cache_kt, cache_v2, lambda_qk[l], q_b, k_b, v_b, lam_init)
        xs = _output(_to_time_major(o_b.astype(_BF16)), gd, sd, mabc, xs, gate, g_subln[l], w_br_b, w_o_b, g_post[l],
                     layer=l, tm=ds * grp, lam_init=lam_init)
        for lst, t in zip(outs_s, (k_b.reshape(db, ds, N_HEADS, 2, HEAD_DK), v_b.reshape(db, ds, N_HEADS, HEAD_DV),
                                   _from_time_major(pst, POOL_BUF), _from_time_major(sst, SCONV_K - 1),
                                   _from_time_major(cst, CCONV_K - 1))):
            lst.append(t)

    y_sample = _from_time_major(xs, ds)
    v_all, kt_all = kv_stack
    k_prompt = kt_all.reshape(depth, bp, N_HEADS, 2, HEAD_DK, sp).transpose(0, 1, 5, 2, 3, 4)
    v_prompt = v_all.reshape(depth, bp, sp, N_HEADS, HEAD_DV)
    stacked_p = [jnp.stack(t) for t in outs_p]
    stacked_s = [jnp.stack(t) for t in outs_s]
    return (xp, y_sample, k_prompt, v_prompt, *stacked_p, *stacked_s)
```

```python
import functools
import math

import jax
import jax.numpy as jnp
from jax import lax
from jax.experimental import pallas as pl
from jax.experimental.pallas import tpu as pltpu

D_MODEL = 1024
N_BRANCH = 4
BRANCH_W = D_MODEL // 2
POOL_WINDOWS = (2, 4, 8, 16)
POOL_GC = BRANCH_W // len(POOL_WINDOWS)
POOL_BUF = max(POOL_WINDOWS) - 1
SCONV_K = 3
CCONV_K = 31
HEAD_DK = 64
HEAD_DV = 2 * HEAD_DK
N_HEADS = BRANCH_W // HEAD_DV
ATTN_SCALE = HEAD_DK ** -0.5
ROPE_THETA = 10000.0
EPS = 1e-6

_C_POOL = 0
_C_BG = BRANCH_W
_C_CG = 2 * BRANCH_W
_C_HB = 3 * BRANCH_W
_C_VAL = 4 * BRANCH_W
_C_GL = 5 * BRANCH_W
_C_Q = 6 * BRANCH_W
_C_K = 7 * BRANCH_W
_C_V = 8 * BRANCH_W
_C_MGATE = 9 * BRANCH_W
_C_MRG = _C_MGATE + N_BRANCH * BRANCH_W
N_IN = _C_MRG + N_BRANCH * D_MODEL

_LANES = 128
_SUBLANES = 8
_VMEM_LIMIT = 56 * 1024 * 1024

_PROMPT_TILE = 256
_OUTPUT_TILE = 1024
_ATTN_TILE = 256
_ATTN_SUBTILES = 4
_SAMPLE_GROUP = 32

_BF16 = jnp.bfloat16
_F32 = jnp.float32
_LOG2E = math.log2(math.e)


def _rms(x, g):
    return x * lax.rsqrt(jnp.mean(x * x, axis=-1, keepdims=True) + EPS) * g


def _sigmoid(x):
    return 0.5 * jnp.tanh(0.5 * x) + 0.5


def _silu(x):
    return x * _sigmoid(x)


def _dot(a, b):
    return jnp.dot(a.astype(_BF16), b.astype(_BF16), preferred_element_type=_F32)


def _dot_nt(a, b):
    return lax.dot_general(a.astype(_BF16), b.astype(_BF16), (((1,), (1,)), ((), ())),
                           preferred_element_type=_F32)


def _lambda(lq_ref, lam_init):
    lq = lq_ref[...]
    a = jnp.sum(lq[0:1, :] * lq[1:2, :], axis=-1, keepdims=True)
    b = jnp.sum(lq[2:3, :] * lq[3:4, :], axis=-1, keepdims=True)
    return jnp.exp(a) - jnp.exp(b) + lam_init


def _mod_kernel(c_ref, w_ref, b_ref, o_ref):
    o_ref[...] = _dot(_silu(c_ref[...]), w_ref[...]) + b_ref[...]


def _modulation(c_all, w_ada, b_ada):
    depth = w_ada.shape[0]
    rows = c_all.shape[0]
    n_blk = 3
    return pl.pallas_call(
        _mod_kernel,
        grid=(depth, n_blk),
        in_specs=[
            pl.BlockSpec((rows, D_MODEL), lambda l, j: (0, 0)),
            pl.BlockSpec((None, D_MODEL, D_MODEL), lambda l, j: (l, 0, j)),
            pl.BlockSpec((None, 1, D_MODEL), lambda l, j: (l, 0, j)),
        ],
        out_specs=pl.BlockSpec((None, rows, D_MODEL), lambda l, j: (l, 0, j)),
        out_shape=jax.ShapeDtypeStruct((depth, rows, 3 * D_MODEL), _F32),
        compiler_params=pltpu.CompilerParams(dimension_semantics=("arbitrary", "arbitrary"),
                                             vmem_limit_bytes=_VMEM_LIMIT),
        name="adaln_mod",
    )(c_all, w_ada, b_ada.reshape(depth, 1, 3 * D_MODEL))


def _halo_steps(k_minus_1, stride):
    if stride == 1:
        return -(-k_minus_1 // _SUBLANES) * _SUBLANES
    return k_minus_1


_MIXER_INPUTS = 18


def _mixer_kernel(*refs, tm, stride, pos_base, n_tiles, n_alias, stacked):
    (x_ref, shift_ref, scale_ref, gpre_ref, win_ref, wpool_ref, pscale_ref, wsc_ref, wcc_ref, bcc_ref,
     gcn_ref, bcn_ref, wbr_ref, cos_ref, sin_ref, ppre_ref, spre_ref, cpre_ref) = refs[:_MIXER_INPUTS]
    outs = refs[_MIXER_INPUTS + n_alias:]
    q_ref, k_ref, v_ref, mabc_ref, gd_ref, sd_ref, pst_ref, sst_ref, cst_ref = outs[:9]
    n_out = 10 if stacked else 9
    extp, exts, extc, hbuf = outs[n_out:]
    i = pl.program_id(1)
    hp = _halo_steps(POOL_BUF, stride) * stride
    hs = _halo_steps(SCONV_K - 1, stride) * stride
    hc = _halo_steps(CCONV_K - 1, stride) * stride

    @pl.when(i == 0)
    def _():
        extp[0:hp, :] = ppre_ref[...]
        exts[0:hs, :] = spre_ref[...]
        extc[0:hc, :] = cpre_ref[...]

    x = x_ref[...]
    h = _rms(x, gpre_ref[...]) * (1.0 + scale_ref[...]) + shift_ref[...]
    hb = h.astype(_BF16)

    def proj(c0, n):
        return jnp.dot(hb, win_ref[:, c0:c0 + n], preferred_element_type=_F32)

    def gated_branch(y, n):
        ys = y * _silu(proj(_C_MGATE + n * BRANCH_W, BRANCH_W))
        br = _dot(ys, wbr_ref[n])
        return _sigmoid(proj(_C_MRG + n * D_MODEL, D_MODEL)) * br

    u_a = proj(_C_POOL, BRANCH_W)
    extp[hp:hp + tm, :] = u_a
    row = lax.broadcasted_iota(jnp.int32, (tm, 1), 0) + i * tm
    step = row if stride == 1 else lax.shift_right_logical(row, int(math.log2(stride)))
    pos = pos_base + step
    d_parts = []
    for g, win in enumerate(POOL_WINDOWS):
        sl = slice(g * POOL_GC, (g + 1) * POOL_GC)
        cur = extp[hp:hp + tm, sl]
        acc = cur
        for d in range(1, win):
            acc = acc + extp[hp - d * stride:hp - d * stride + tm, sl]
        cnt = jnp.minimum(win, pos + 1).astype(_F32)
        d_parts.append(acc / cnt - cur)
    d_pool = jnp.concatenate(d_parts, axis=-1)
    y_a = _dot(d_pool, wpool_ref[...]) * pscale_ref[...]
    merged = gated_branch(y_a, 0)

    z_b = proj(_C_CG, BRANCH_W) * proj(_C_HB, BRANCH_W)
    exts[hs:hs + tm, :] = z_b
    conv_b = wsc_ref[SCONV_K - 1:SCONV_K, :] * z_b
    for j in range(SCONV_K - 1):
        off = hs - (SCONV_K - 1 - j) * stride
        conv_b = conv_b + wsc_ref[j:j + 1, :] * exts[off:off + tm, :]
    y_b = proj(_C_BG, BRANCH_W) * conv_b
    merged = merged + gated_branch(y_b, 1)

    z_c = proj(_C_VAL, BRANCH_W) * _sigmoid(proj(_C_GL, BRANCH_W))
    extc[hc:hc + tm, :] = z_c
    groups = {}
    for j in range(CCONV_K):
        off = hc - (CCONV_K - 1 - j) * stride
        groups.setdefault(off % _SUBLANES, []).append((j, off - off % _SUBLANES))
    conv_parts = []
    for ci, c0 in enumerate(range(0, BRANCH_W, _LANES)):
        sl = slice(c0, c0 + _LANES)
        out = None
        for rem, taps in sorted(groups.items()):
            n = tm if rem == 0 else tm + _SUBLANES
            acc = None
            for j, base in taps:
                term = wcc_ref[j:j + 1, sl] * extc[base:base + n, sl]
                acc = term if acc is None else acc + term
            if rem:
                hbuf[ci, rem - 1] = acc
                acc = hbuf[ci, rem - 1, rem:rem + tm, :]
            out = acc if out is None else out + acc
        conv_parts.append(out)
    conv_c = jnp.concatenate(conv_parts, axis=-1) + bcc_ref[...]
    mu = jnp.mean(conv_c, axis=-1, keepdims=True)
    xc = conv_c - mu
    ln = xc * lax.rsqrt(jnp.mean(xc * xc, axis=-1, keepdims=True) + EPS) * gcn_ref[...] + bcn_ref[...]
    y_c = _silu(ln)
    merged = merged + gated_branch(y_c, 2)
    mabc_ref[...] = merged.astype(mabc_ref.dtype)

    cos = jnp.concatenate([cos_ref[...]] * (BRANCH_W // _LANES), axis=-1)
    sin = jnp.concatenate([sin_ref[...]] * (BRANCH_W // _LANES), axis=-1)
    lane = lax.broadcasted_iota(jnp.int32, (tm, BRANCH_W), 1)
    first_half = (lane % HEAD_DK) < (HEAD_DK // 2)

    def rope(t):
        fwd = pltpu.roll(t, BRANCH_W - HEAD_DK // 2, axis=1)
        bwd = pltpu.roll(t, HEAD_DK // 2, axis=1)
        return t * cos + jnp.where(first_half, fwd, bwd) * sin

    q_ref[...] = (rope(proj(_C_Q, BRANCH_W)) * (ATTN_SCALE * _LOG2E)).astype(q_ref.dtype)
    k_rot = rope(proj(_C_K, BRANCH_W))
    k_ref[...] = k_rot.astype(k_ref.dtype)
    v_new = proj(_C_V, BRANCH_W)
    if stacked:
        outs[9][...] = k_rot.T
        for hd in range(N_HEADS):
            v_ref[pl.ds(hd, tm, stride=N_HEADS), :] = v_new[:, hd * HEAD_DV:(hd + 1) * HEAD_DV]
    else:
        v_ref[...] = v_new
    gd_ref[...] = _silu(proj(_C_MGATE + 3 * BRANCH_W, BRANCH_W)).astype(gd_ref.dtype)
    sd_ref[...] = _sigmoid(proj(_C_MRG + 3 * D_MODEL, D_MODEL)).astype(sd_ref.dtype)

    new_p = extp[tm:tm + hp, :]
    new_s = exts[tm:tm + hs, :]
    new_c = extc[tm:tm + hc, :]
    if n_tiles > 1:
        extp[0:hp, :] = new_p
        exts[0:hs, :] = new_s
        extc[0:hc, :] = new_c

    @pl.when(i == n_tiles - 1)
    def _():
        pst_ref[...] = new_p
        sst_ref[...] = new_s
        cst_ref[...] = new_c


def _mod_spec(mod, layer, tm, part):
    if mod.shape[2] == 1:
        return pl.BlockSpec((None, None, 1, D_MODEL), lambda b, i: (layer, b, 0, part))
    return pl.BlockSpec((None, None, tm, D_MODEL), lambda b, i: (layer, b, i, part))


def _mixer(x, mod, g_pre, w_in, w_pool_bd, pool_scale, w_sconv, w_cconv, b_cconv, g_cnorm,
           b_cnorm, w_branch, cos, sin, pool_pre, sconv_pre, cconv_pre, *, layer, pre_layer, tm, stride,
           pos_base, kv_stack=None):
    depth = w_in.shape[0]
    nb, rows, _ = x.shape
    n_tiles = rows // tm
    hp = _halo_steps(POOL_BUF, stride) * stride
    hs = _halo_steps(SCONV_K - 1, stride) * stride
    hc = _halo_steps(CCONV_K - 1, stride) * stride
    once = pl.Buffered(1)

    def const(shape):
        nd = len(shape)
        return pl.BlockSpec(shape, lambda b, i: (0,) * nd, pipeline_mode=once)

    def per_layer(shape):
        nd = len(shape)
        return pl.BlockSpec((None,) + shape, lambda b, i: (layer,) + (0,) * nd, pipeline_mode=once)

    def per_b(shape):
        return pl.BlockSpec((None,) + shape, lambda b, i: (b, 0, 0))

    def tiled(width):
        return pl.BlockSpec((None, tm, width), lambda b, i: (b, i, 0))

    def history(rows_kept):
        return pl.BlockSpec((None, None, rows_kept, BRANCH_W), lambda b, i: (pre_layer, b, 0, 0))

    row_vec = lambda a: a.reshape(1, -1)
    stacked = kv_stack is not None
    n_alias = len(kv_stack) if stacked else 0
    kern = functools.partial(_mixer_kernel, tm=tm, stride=stride, pos_base=pos_base, n_tiles=n_tiles,
                             n_alias=n_alias, stacked=stacked)
    out_shapes = [jax.ShapeDtypeStruct((nb, rows, w), dt)
                  for w, dt in ((BRANCH_W, _BF16), (BRANCH_W, _BF16 if stacked else _F32), (BRANCH_W, _F32),
                                (D_MODEL, _BF16), (BRANCH_W, _BF16), (D_MODEL, _BF16))]
    out_shapes += [jax.ShapeDtypeStruct((nb, hh, BRANCH_W), _F32) for hh in (hp, hs, hc)]
    out_specs = [tiled(BRANCH_W), tiled(BRANCH_W), tiled(BRANCH_W), tiled(D_MODEL), tiled(BRANCH_W),
                 tiled(D_MODEL), per_b((hp, BRANCH_W)), per_b((hs, BRANCH_W)), per_b((hc, BRANCH_W))]
    aliases = {}
    if stacked:
        out_shapes[2] = jax.ShapeDtypeStruct((depth, nb, rows * N_HEADS, HEAD_DV), _F32)
        out_specs[2] = pl.BlockSpec((None, None, tm * N_HEADS, HEAD_DV), lambda b, i: (layer, b, i, 0))
        out_shapes.append(jax.ShapeDtypeStruct((depth, nb, BRANCH_W, rows), _F32))
        out_specs.append(pl.BlockSpec((None, None, BRANCH_W, tm), lambda b, i: (layer, b, 0, i)))
        if n_alias:
            aliases = {_MIXER_INPUTS: 2, _MIXER_INPUTS + 1: 9}
    return pl.pallas_call(
        kern,
        grid=(nb, n_tiles),
        in_specs=[
            tiled(D_MODEL), _mod_spec(mod, layer, tm, 0), _mod_spec(mod, layer, tm, 1), const((1, D_MODEL)),
            per_layer((D_MODEL, N_IN)), const((BRANCH_W, BRANCH_W)), const((1, BRANCH_W)),
            const((SCONV_K, BRANCH_W)), const((CCONV_K, BRANCH_W)), const((1, BRANCH_W)),
            const((1, BRANCH_W)), const((1, BRANCH_W)), per_layer((N_BRANCH, BRANCH_W, D_MODEL)),
            pl.BlockSpec((tm, _LANES), lambda b, i: (i, 0)), pl.BlockSpec((tm, _LANES), lambda b, i: (i, 0)),
            history(hp), history(hs), history(hc),
        ] + [pl.BlockSpec(memory_space=pl.ANY)] * n_alias,
        out_specs=out_specs,
        out_shape=out_shapes,
        input_output_aliases=aliases,
        scratch_shapes=[pltpu.VMEM((hp + tm, BRANCH_W), _F32), pltpu.VMEM((hs + tm, BRANCH_W), _F32),
                        pltpu.VMEM((hc + tm, BRANCH_W), _F32),
                        pltpu.VMEM((BRANCH_W // _LANES, _SUBLANES - 1, tm + _SUBLANES, _LANES), _F32)],
        compiler_params=pltpu.CompilerParams(dimension_semantics=("arbitrary", "arbitrary"),
                                             vmem_limit_bytes=_VMEM_LIMIT),
        name="mixer_s%d" % stride,
    )(x, mod, mod, row_vec(g_pre), w_in, w_pool_bd, row_vec(pool_scale), w_sconv, w_cconv,
      row_vec(b_cconv), row_vec(g_cnorm), row_vec(b_cnorm), w_branch, cos, sin,
      pool_pre, sconv_pre, cconv_pre, *(kv_stack or ()))


def _attn_prompt_kernel(lq_ref, q_ref, k_ref, v_ref, o_ref, vt_ref, qs_ref, acc_ref, m_ref, l_ref, st_ref, *,
                        tq, n_sub, lam_init):
    i = pl.program_id(1)
    n_kv = k_ref.shape[0] // tq
    lam = _lambda(lq_ref, lam_init)

    @pl.when(i == 0)
    def _():
        for h in range(N_HEADS):
            for c in range(n_kv):
                vt_ref[h, c] = v_ref[pl.ds(c * tq * N_HEADS + h, tq, stride=N_HEADS), :].T.astype(_BF16)

    lane = lax.broadcasted_iota(jnp.int32, (tq, HEAD_DV), 1)
    first = lane < HEAD_DK
    for s in range(n_sub):
        for h in range(N_HEADS):
            qh = q_ref[s * tq:(s + 1) * tq, h * HEAD_DV:(h + 1) * HEAD_DV].astype(_F32)
            qs_ref[s * N_HEADS + h, 0:tq, :] = jnp.where(first, qh, 0.0).astype(_BF16)
            qs_ref[s * N_HEADS + h, tq:2 * tq, :] = jnp.where(first, 0.0, qh).astype(_BF16)
    acc_ref[...] = jnp.zeros_like(acc_ref)
    m_ref[...] = jnp.full(m_ref.shape, -jnp.inf, _F32)
    l_ref[...] = jnp.zeros_like(l_ref)
    causal = (lax.broadcasted_iota(jnp.int32, (tq, 2 * tq), 0)
              <= lax.broadcasted_iota(jnp.int32, (tq, 2 * tq), 1) % tq)

    def tile(j, subs):
        r0 = pl.multiple_of(j * tq, tq)
        for h in range(N_HEADS):
            kj = k_ref[pl.ds(r0, tq), h * HEAD_DV:(h + 1) * HEAD_DV].astype(_BF16)
            for s, _ in subs:
                st_ref[s * N_HEADS + h] = _dot_nt(kj, qs_ref[s * N_HEADS + h])
        for h in range(N_HEADS):
            vtj = vt_ref[h, j]
            for s, masked in subs:
                c = s * N_HEADS + h
                st = st_ref[c]
                if masked:
                    st = jnp.where(causal, st, -jnp.inf)
                m = m_ref[c:c + 1, :]
                m_new = jnp.maximum(m, jnp.max(st, axis=0, keepdims=True))
                p = jnp.exp2(st - m_new)
                alpha = jnp.exp2(m - m_new)
                m_ref[c:c + 1, :] = m_new
                l_ref[c:c + 1, :] = alpha * l_ref[c:c + 1, :] + jnp.sum(p, axis=0, keepdims=True)
                acc_ref[c] = alpha * acc_ref[c] + jnp.dot(vtj, p.astype(_BF16), preferred_element_type=_F32)

    def body(j, carry):
        tile(j, [(s, False) for s in range(n_sub)])
        return carry

    lax.fori_loop(0, n_sub * i, body, 0)
    for t in range(n_sub):
        tile(n_sub * i + t, [(t, True)] + [(s, False) for s in range(t + 1, n_sub)])
    for s in range(n_sub):
        for h in range(N_HEADS):
            c = s * N_HEADS + h
            a = acc_ref[c] * (1.0 / l_ref[c:c + 1, :])
            o_t = a[:, 0:tq] - lam * a[:, tq:2 * tq]
            o_ref[s * tq:(s + 1) * tq, h * HEAD_DV:(h + 1) * HEAD_DV] = o_t.T.astype(o_ref.dtype)


def _attn_prompt(lambda_qk_l, q, k, v_all, layer, lam_init):
    nb, s, _ = q.shape
    tq = _ATTN_TILE
    n_sub = _ATTN_SUBTILES
    n_chain = n_sub * N_HEADS
    kern = functools.partial(_attn_prompt_kernel, tq=tq, n_sub=n_sub, lam_init=lam_init)
    return pl.pallas_call(
        kern,
        grid=(nb, s // (n_sub * tq)),
        in_specs=[
            pl.BlockSpec((4, HEAD_DK), lambda b, i: (0, 0)),
            pl.BlockSpec((None, n_sub * tq, BRANCH_W), lambda b, i: (b, i, 0)),
            pl.BlockSpec((None, s, BRANCH_W), lambda b, i: (b, 0, 0)),
            pl.BlockSpec((None, None, s * N_HEADS, HEAD_DV), lambda b, i: (layer, b, 0, 0)),
        ],
        out_specs=pl.BlockSpec((None, n_sub * tq, BRANCH_W), lambda b, i: (b, i, 0)),
        out_shape=jax.ShapeDtypeStruct((nb, s, BRANCH_W), _BF16),
        scratch_shapes=[pltpu.VMEM((N_HEADS, s // tq, HEAD_DV, tq), _BF16),
                        pltpu.VMEM((n_chain, 2 * tq, HEAD_DV), _BF16),
                        pltpu.VMEM((n_chain, HEAD_DV, 2 * tq), _F32),
                        pltpu.VMEM((n_chain, 2 * tq), _F32), pltpu.VMEM((n_chain, 2 * tq), _F32),
                        pltpu.VMEM((n_chain, tq, 2 * tq), _F32)],
        compiler_params=pltpu.CompilerParams(dimension_semantics=("arbitrary", "arbitrary"),
                                             vmem_limit_bytes=_VMEM_LIMIT),
        name="attn_prompt",
    )(lambda_qk_l, q, k, v_all)


def _attn_sample_kernel(pt_ref, lq_ref, q_ref, kn_ref, vn_ref, *rest, n_pages, page, t_new, lam_init):
    del pt_ref
    k_pages = rest[:n_pages]
    v_pages = rest[n_pages:2 * n_pages]
    o_ref = rest[2 * n_pages]
    s_ref, knew, vnew = rest[2 * n_pages + 1:]
    lam = _lambda(lq_ref, lam_init)
    n_rows = 2 * N_HEADS * t_new
    half = N_HEADS * t_new

    r = lax.broadcasted_iota(jnp.int32, (n_rows, BRANCH_W), 0)
    c = lax.broadcasted_iota(jnp.int32, (n_rows, BRANCH_W), 1)
    q_rep = jnp.zeros((n_rows, BRANCH_W), _F32)
    for t in range(t_new):
        q_rep = jnp.where(r % t_new == t, q_ref[t:t + 1, :], q_rep)
    chunk_of_row = 2 * ((r % half) // t_new) + r // half
    wq = jnp.where(c // HEAD_DK == chunk_of_row, q_rep, 0.0).astype(_BF16)

    for p in range(n_pages):
        s_ref[:, p * page:(p + 1) * page] = _dot(wq, k_pages[p][...])

    @pl.when(pl.program_id(0) == 0)
    def _():
        knew[...] = jnp.zeros_like(knew)
        vnew[...] = jnp.zeros_like(vnew)

    knew[0:t_new, :] = kn_ref[...]
    vnew[0:t_new, :] = vn_ref[...]
    s_new = _dot_nt(wq, knew[...])
    key = lax.broadcasted_iota(jnp.int32, (n_rows, page), 1)
    tok = lax.broadcasted_iota(jnp.int32, (n_rows, page), 0) % t_new
    s_ref[:, n_pages * page:(n_pages + 1) * page] = jnp.where(key <= tok, s_new, -jnp.inf)

    s = s_ref[...]
    e = jnp.exp2(s - jnp.max(s, axis=-1, keepdims=True))
    pr = e / jnp.sum(e, axis=-1, keepdims=True)
    wts = (pr[0:half, :] - lam * pr[half:n_rows, :]).astype(_BF16)

    acc = jnp.dot(wts[:, n_pages * page:(n_pages + 1) * page], vnew[...].astype(_BF16),
                  preferred_element_type=_F32)
    for p in range(n_pages):
        v_cat = jnp.concatenate([v_pages[p][pl.ds(h, page, stride=N_HEADS), :] for h in range(N_HEADS)],
                                axis=-1).astype(_BF16)
        acc = acc + jnp.dot(wts[:, p * page:(p + 1) * page], v_cat, preferred_element_type=_F32)
    ar = lax.broadcasted_iota(jnp.int32, (half, BRANCH_W), 0)
    ac = lax.broadcasted_iota(jnp.int32, (half, BRANCH_W), 1)
    own = jnp.where(ar // t_new == ac // HEAD_DV, acc, 0.0)
    tot = own
    for h in range(1, N_HEADS):
        tot = tot + pltpu.roll(own, h * t_new, axis=0)
    o_ref[...] = tot[0:t_new, :]


def _attn_sample(layer, page_table, cache_kt, cache_v, lambda_qk_l, q, k_new, v_new, lam_init):
    nb, t_new, _ = q.shape
    n_pages = page_table.shape[1]
    page = cache_kt.shape[3]

    def k_spec(j):
        return pl.BlockSpec((None, None, BRANCH_W, page), lambda b, pt: (layer, pt[b, j], 0, 0))

    def v_spec(j):
        return pl.BlockSpec((None, None, N_HEADS * page, HEAD_DV), lambda b, pt: (layer, pt[b, j], 0, 0))

    tok_spec = pl.BlockSpec((None, t_new, BRANCH_W), lambda b, pt: (b, 0, 0))
    kern = functools.partial(_attn_sample_kernel, n_pages=n_pages, page=page, t_new=t_new, lam_init=lam_init)
    n_rows = 2 * N_HEADS * t_new
    grid_spec = pltpu.PrefetchScalarGridSpec(
        num_scalar_prefetch=1,
        grid=(nb,),
        in_specs=[pl.BlockSpec((4, HEAD_DK), lambda b, pt: (0, 0)), tok_spec, tok_spec, tok_spec]
        + [k_spec(j) for j in range(n_pages)] + [v_spec(j) for j in range(n_pages)],
        out_specs=tok_spec,
        scratch_shapes=[pltpu.VMEM((n_rows, (n_pages + 1) * page), _F32),
                        pltpu.VMEM((page, BRANCH_W), _F32), pltpu.VMEM((page, BRANCH_W), _F32)],
    )
    return pl.pallas_call(
        kern,
        grid_spec=grid_spec,
        out_shape=jax.ShapeDtypeStruct((nb, t_new, BRANCH_W), _F32),
        compiler_params=pltpu.CompilerParams(dimension_semantics=("arbitrary",),
                                             vmem_limit_bytes=_VMEM_LIMIT),
        name="attn_sample",
    )(page_table, lambda_qk_l, q, k_new, v_new, *([cache_kt] * n_pages), *([cache_v] * n_pages))


def _output_kernel(o_ref, gd_ref, sd_ref, mabc_ref, x_ref, gate_ref, gsub_ref, wbr_ref, wo_ref, gpost_ref,
                   y_ref, *, lam_init):
    o = o_ref[...].astype(_F32)
    parts = [_rms(o[:, h * HEAD_DV:(h + 1) * HEAD_DV], gsub_ref[...]) * (1.0 - lam_init)
             for h in range(N_HEADS)]
    ys = jnp.concatenate(parts, axis=-1) * gd_ref[...].astype(_F32)
    merged = mabc_ref[...].astype(_F32) + sd_ref[...].astype(_F32) * _dot(ys, wbr_ref[...])
    out = _rms(_dot(merged, wo_ref[...]), gpost_ref[...])
    y_ref[...] = x_ref[...] + gate_ref[...] * out


def _output(o, gd, sd, mabc, x, mod, g_subln, w_branch, w_o, g_post, *, layer, tm, lam_init):
    nb, rows, _ = x.shape
    once = pl.Buffered(1)

    def tiled(width):
        return pl.BlockSpec((None, tm, width), lambda b, i: (b, i, 0))

    gate_spec = _mod_spec(mod, layer, tm, 2)
    kern = functools.partial(_output_kernel, lam_init=lam_init)
    return pl.pallas_call(
        kern,
        grid=(nb, rows // tm),
        in_specs=[
            tiled(BRANCH_W), tiled(BRANCH_W), tiled(D_MODEL), tiled(D_MODEL), tiled(D_MODEL), gate_spec,
            pl.BlockSpec((1, HEAD_DV), lambda b, i: (0, 0)),
            pl.BlockSpec((None, None, BRANCH_W, D_MODEL), lambda b, i: (layer, N_BRANCH - 1, 0, 0),
                         pipeline_mode=once),
            pl.BlockSpec((None, D_MODEL, D_MODEL), lambda b, i: (layer, 0, 0), pipeline_mode=once),
            pl.BlockSpec((1, D_MODEL), lambda b, i: (0, 0)),
        ],
        out_specs=tiled(D_MODEL),
        out_shape=jax.ShapeDtypeStruct((nb, rows, D_MODEL), _F32),
        compiler_params=pltpu.CompilerParams(dimension_semantics=("arbitrary", "arbitrary"),
                                             vmem_limit_bytes=_VMEM_LIMIT),
        name="output_proj",
    )(o, gd, sd, mabc, x, mod, g_subln.reshape(1, -1), w_branch, w_o, g_post.reshape(1, -1))


def _to_time_major(a):
    *lead, nb, t, c = a.shape
    g, n = nb // _SAMPLE_GROUP, len(lead)
    a = a.reshape(*lead, g, _SAMPLE_GROUP, t, c).transpose(*range(n), n, n + 2, n + 1, n + 3)
    return a.reshape(*lead, g, t * _SAMPLE_GROUP, c)


def _from_time_major(a, t):
    *lead, g, _, c = a.shape
    n = len(lead)
    a = a.reshape(*lead, g, t, _SAMPLE_GROUP, c).transpose(*range(n), n, n + 2, n + 1, n + 3)
    return a.reshape(*lead, g * _SAMPLE_GROUP, t, c)


def _rope_tables(pos):
    half = HEAD_DK // 2
    inv = jnp.power(ROPE_THETA, -jnp.arange(half, dtype=_F32) / half)
    ang = pos.astype(_F32)[:, None] * inv[None, :]
    cos, sin = jnp.cos(ang), jnp.sin(ang)
    reps = _LANES // HEAD_DK
    cos_t = jnp.tile(jnp.concatenate([cos, cos], axis=-1), (1, reps))
    sin_t = jnp.tile(jnp.concatenate([-sin, sin], axis=-1), (1, reps))
    return cos_t, sin_t


def _block_diag(w):
    g, a, b = w.shape
    out = jnp.zeros((g * a, g * b), w.dtype)
    for n in range(g):
        out = out.at[n * a:(n + 1) * a, n * b:(n + 1) * b].set(w[n])
    return out


def kernel(x_prompt, x_sample, cache_k, cache_v, page_table, state_pool, state_sconv, state_cconv,
           c_prompt, c_sample, w_ada, b_ada, g_pre, g_post, w_in, w_pool, pool_scale, w_sconv, w_cconv,
           b_cconv, g_cnorm, b_cnorm, lambda_qk, g_subln, w_branch, w_o):
    bp, sp, _ = x_prompt.shape
    db, ds, _ = x_sample.shape
    depth = w_in.shape[0]
    n_pool, page = cache_k.shape[1], cache_k.shape[2]
    past = page_table.shape[1] * page
    grp = _SAMPLE_GROUP

    mod = _modulation(jnp.concatenate([c_prompt, c_sample], axis=0), w_ada, b_ada)
    cos_p, sin_p = _rope_tables(jnp.arange(sp))
    cos_s, sin_s = _rope_tables(jnp.repeat(past + jnp.arange(ds), grp))
    cache_kt = cache_k.transpose(0, 1, 3, 4, 5, 2).reshape(depth, n_pool, BRANCH_W, page)
    cache_v2 = cache_v.reshape(depth, n_pool, page * N_HEADS, HEAD_DV)

    hp1, hs1, hc1 = (_halo_steps(n, 1) for n in (POOL_BUF, SCONV_K - 1, CCONV_K - 1))
    zero_pre = [jnp.zeros((1, bp, hh, BRANCH_W), _F32) for hh in (hp1, hs1, hc1)]
    pre_s = [_to_time_major(st) for st in (state_pool, state_sconv, state_cconv)]
    mod_p = mod[:, :bp, None, :]
    mod_s = _to_time_major(jnp.broadcast_to(mod[:, bp:, None, :], (depth, db, ds, 3 * D_MODEL)))
    w_in_b, w_br_b, w_o_b = (w.astype(_BF16) for w in (w_in, w_branch, w_o))

    xp = x_prompt
    xs = _to_time_major(x_sample)
    outs_p = [[] for _ in range(3)]
    outs_s = [[] for _ in range(5)]
    kv_stack = ()
    for l in range(depth):
        lam_init = 0.8 - 0.6 * math.exp(-0.3 * l)
        w_pool_l = _block_diag(w_pool[l]).astype(_BF16)
        weights = (g_pre[l], w_in_b, w_pool_l, pool_scale[l], w_sconv[l], w_cconv[l], b_cconv[l],
                   g_cnorm[l], b_cnorm[l], w_br_b)

        q, k, v_all, mabc, gd, sd, pst, sst, cst, kt_all = _mixer(
            xp, mod_p, *weights, cos_p, sin_p, *zero_pre, layer=l, pre_layer=0, tm=_PROMPT_TILE, stride=1,
            pos_base=0, kv_stack=kv_stack)
        kv_stack = (v_all, kt_all)
        o = _attn_prompt(lambda_qk[l], q, k, v_all, l, lam_init)
        xp = _output(o, gd, sd, mabc, xp, mod_p, g_subln[l], w_br_b, w_o_b, g_post[l],
                     layer=l, tm=_OUTPUT_TILE, lam_init=lam_init)
        for lst, t in zip(outs_p, (pst, sst, cst)):
            lst.append(t)

        q, k, v, mabc, gd, sd, pst, sst, cst = _mixer(
            xs, mod_s, *weights, cos_s, sin_s, *pre_s, layer=l, pre_layer=l, tm=ds * grp, stride=grp,
            pos_base=past)
        q_b, k_b, v_b = (_from_time_major(t, ds) for t in (q.astype(_F32), k, v))
        o_b = _attn_sample(l, page_table, cache_kt, cache_v2, lambda_qk[l], q_b, k_b, v_b, lam_init)
        xs = _output(_to_time_major(o_b.astype(_BF16)), gd, sd, mabc, xs, mod_s, g_subln[l], w_br_b, w_o_b,
                     g_post[l], layer=l, tm=ds * grp, lam_init=lam_init)
        for lst, t in zip(outs_s, (k_b, v_b, pst, sst, cst)):
            lst.append(t)

    y_sample = _from_time_major(xs, ds)
    v_all, kt_all = kv_stack
    k_prompt = kt_all.reshape(depth, bp, N_HEADS, 2, HEAD_DK, sp).transpose(0, 1, 5, 2, 3, 4)
    v_prompt = v_all.reshape(depth, bp, sp, N_HEADS, HEAD_DV)
    pool_p, sconv_p, cconv_p = (jnp.stack(t)[:, :, hh - keep:] for t, hh, keep in
                                zip(outs_p, (hp1, hs1, hc1), (POOL_BUF, SCONV_K - 1, CCONV_K - 1)))
    k_s = jnp.stack(outs_s[0]).reshape(depth, db, ds, N_HEADS, 2, HEAD_DK)
    v_s = jnp.stack(outs_s[1]).reshape(depth, db, ds, N_HEADS, HEAD_DV)
    pool_s, sconv_s, cconv_s = (_from_time_major(jnp.stack(t), keep) for t, keep in
                                zip(outs_s[2:], (POOL_BUF, SCONV_K - 1, CCONV_K - 1)))
    return (xp, y_sample, k_prompt, v_prompt, pool_p, sconv_p, cconv_p, k_s, v_s, pool_s, sconv_s, cconv_s)
```
